```python
import math
import jax, jax.numpy as jnp
from jax import lax
import numpy as np

D_MODEL = 2048
BATCH = 2
SEQ = 16384
DEPTH = 2
DEC_BATCH = 4
DEC_SEQ = 2048
PAST_LEN = 128

N_MIXERS = 2
N_ATTN_LAYERS = (DEPTH + 1) // 2
N_FOURIER_LAYERS = DEPTH // 2
DILATION_PATTERNS = ((128, 1), (512, 4), (2048, 16))
N_DIL_GROUPS = len(DILATION_PATTERNS)
HEAD_DIM = 128
HEADS_PER_GROUP = 8
ATTN_WIDTH = HEADS_PER_GROUP * HEAD_DIM
ROT_DIM = HEAD_DIM // 4
ROPE_THETA = 500000.0
Q_BLOCK = 64
FOURIER_GROUPS = 8
FOURIER_GROUP_DIM = D_MODEL // FOURIER_GROUPS
N_EXPERTS = 64
N_EXPERT_GROUPS = 8
EXPERTS_PER_GROUP = N_EXPERTS // N_EXPERT_GROUPS
TOP_K = 2
D_EXPERT = 1408
EXPERT_BLOCK = 256
LN_EPS = 1e-5
ALPHA = (2 * DEPTH) ** 0.25
BETA = (8 * DEPTH) ** -0.25
NEG_BIG = -1e30

kernel_name = "dilated_fourier_moe_encoder"


def layer_norm(x, g, b):
    xf = x.astype(jnp.float32)
    mu = xf.mean(-1, keepdims=True)
    var = jnp.square(xf - mu).mean(-1, keepdims=True)
    y = (xf - mu) * lax.rsqrt(var + LN_EPS) * g.astype(jnp.float32) + b.astype(jnp.float32)
    return y.astype(x.dtype)


def rotary_tables(seq):
    inv_freq = ROPE_THETA ** (-jnp.arange(0, ROT_DIM, 2, dtype=jnp.float32) / ROT_DIM)
    ang = jnp.arange(seq, dtype=jnp.float32)[:, None] * inv_freq[None, :]
    return jnp.cos(ang), jnp.sin(ang)


def apply_partial_rotary(t, cos, sin):
    half = ROT_DIM // 2
    tr = t[..., :ROT_DIM].astype(jnp.float32)
    x1, x2 = tr[..., :half], tr[..., half:]
    c = cos[None, :, None, :]
    s = sin[None, :, None, :]
    rot = jnp.concatenate([x1 * c - x2 * s, x2 * c + x1 * s], axis=-1).astype(t.dtype)
    return jnp.concatenate([rot, t[..., ROT_DIM:]], axis=-1)


def band_attention(q, k, v, half):
    N, L, H, E = q.shape
    nb = -(-L // Q_BLOCK)
    Lp = nb * Q_BLOCK
    span = Q_BLOCK + 2 * half
    qb = jnp.pad(q, ((0, 0), (0, Lp - L), (0, 0), (0, 0))).reshape(N, nb, Q_BLOCK, H, E)
    kv_pad = ((0, 0), (half, Lp - L + half), (0, 0), (0, 0))
    kp = jnp.pad(k, kv_pad)
    vp = jnp.pad(v, kv_pad)
    idx = (jnp.arange(nb) * Q_BLOCK)[:, None] + jnp.arange(span)[None, :]
    kb = kp[:, idx]
    vb = vp[:, idx]
    key_pos = idx - half
    key_ok = (key_pos >= 0) & (key_pos < L)
    rel = jnp.arange(span)[None, :] - jnp.arange(Q_BLOCK)[:, None]
    band = (rel >= 0) & (rel <= 2 * half)
    mask = band[None, :, :] & key_ok[:, None, :]
    s = jnp.einsum('nbqhe,nbkhe->nbhqk', qb, kb, preferred_element_type=jnp.float32)
    s = jnp.where(mask[None, :, None, :, :], s, NEG_BIG)
    m = s.max(-1, keepdims=True)
    p = jnp.exp(s - m)
    den = p.sum(-1)
    o = jnp.einsum('nbhqk,nbkhe->nbqhe', p, vb.astype(jnp.float32))
    o = o / den.transpose(0, 1, 3, 2)[..., None]
    lse = (m[..., 0] + jnp.log(den)).transpose(0, 1, 3, 2)
    return o.reshape(N, Lp, H, E)[:, :L], lse.reshape(N, Lp, H)[:, :L]


def dilated_window_attention(q, k, v, window, dilation):
    B, S, H, E = q.shape
    half = window // (2 * dilation)
    L = S // dilation

    def split(t):
        return t.reshape(B, L, dilation, H, E).transpose(0, 2, 1, 3, 4).reshape(B * dilation, L, H, E)

    o, lse = band_attention(split(q), split(k), split(v), half)
    o = o.reshape(B, dilation, L, H, E).transpose(0, 2, 1, 3, 4).reshape(B, S, H, E)
    lse = lse.reshape(B, dilation, L, H).transpose(0, 2, 1, 3).reshape(B, S, H)
    return o, lse


def dilated_mixture_attention(h, w_in, w_out, cos, sin):
    B, S, _ = h.shape
    proj = (h @ w_in).reshape(B, S, N_DIL_GROUPS, 3, HEADS_PER_GROUP, HEAD_DIM)
    outs, lses = [], []
    for g, (window, dilation) in enumerate(DILATION_PATTERNS):
        q = apply_partial_rotary(proj[:, :, g, 0], cos, sin) * (HEAD_DIM ** -0.5)
        k = apply_partial_rotary(proj[:, :, g, 1], cos, sin)
        v = proj[:, :, g, 2]
        o, lse = dilated_window_attention(q, k, v, window, dilation)
        outs.append(o)
        lses.append(lse)
    wts = jax.nn.softmax(jnp.stack(lses, axis=0), axis=0)
    o = sum(wts[g][..., None] * outs[g] for g in range(N_DIL_GROUPS))
    return o.reshape(B, S, ATTN_WIDTH).astype(h.dtype) @ w_out


def fourier_mixer(h, w_out):
    B, S, D = h.shape
    hg = h.astype(jnp.float32).reshape(B, S, FOURIER_GROUPS, FOURIER_GROUP_DIM)
    mixed = jnp.fft.fftn(hg, axes=(1, 3), norm='ortho').real
    return mixed.reshape(B, S, D).astype(h.dtype) @ w_out


def grouped_moe(h, router_w, router_b, w_gate, w_up, w_down):
    B, S, D = h.shape
    T = B * S
    x2d = h.reshape(T, D)
    logits = jnp.einsum('td,de->te', x2d.astype(jnp.float32), router_w.astype(jnp.float32))
    scores = jax.nn.softmax(logits + router_b.astype(jnp.float32), axis=-1)
    group_score = scores.reshape(T, N_EXPERT_GROUPS, EXPERTS_PER_GROUP).max(-1)
    g_star = jnp.argmax(group_score, axis=-1)
    in_group = (jnp.arange(N_EXPERTS) // EXPERTS_PER_GROUP)[None, :] == g_star[:, None]
    top_val, top_idx = lax.top_k(jnp.where(in_group, scores, -1.0), TOP_K)
    gates = top_val / top_val.sum(-1, keepdims=True)

    A = T * TOP_K
    n_blocks = -(-A // EXPERT_BLOCK) + N_EXPERTS
    flat_e = top_idx.reshape(A).astype(jnp.int32)
    flat_tok = jnp.repeat(jnp.arange(T, dtype=jnp.int32), TOP_K)
    flat_gate = gates.reshape(A)
    order = jnp.argsort(flat_e)
    e_sorted = flat_e[order]
    counts = jnp.bincount(flat_e, length=N_EXPERTS)
    padded = -(-counts // EXPERT_BLOCK) * EXPERT_BLOCK
    pad_end = jnp.cumsum(padded)
    pad_start = pad_end - padded
    start = jnp.cumsum(counts) - counts
    dest = pad_start[e_sorted] + (jnp.arange(A) - start[e_sorted])
    buf_tok = jnp.full((n_blocks * EXPERT_BLOCK,), T, jnp.int32).at[dest].set(flat_tok[order])
    buf_gate = jnp.zeros((n_blocks * EXPERT_BLOCK,), jnp.float32).at[dest].set(flat_gate[order])
    block_expert = jnp.searchsorted(pad_end, jnp.arange(n_blocks) * EXPERT_BLOCK, side='right')
    block_expert = jnp.minimum(block_expert, N_EXPERTS - 1).astype(jnp.int32)

    xpad = jnp.concatenate([x2d, jnp.zeros((1, D), x2d.dtype)], axis=0)
    xb = xpad[buf_tok].reshape(n_blocks, EXPERT_BLOCK, D)

    def expert_block(args):
        xblk, e = args
        a = xblk @ w_gate[e]
        u = xblk @ w_up[e]
        return (jax.nn.silu(a) * u) @ w_down[e]

    yb = lax.map(expert_block, (xb, block_expert)).reshape(n_blocks * EXPERT_BLOCK, D)
    y = jnp.zeros((T + 1, D), jnp.float32).at[buf_tok].add(yb.astype(jnp.float32) * buf_gate[:, None])
    return y[:T].astype(h.dtype).reshape(B, S, D)


def trunk(x, ln_in_g, ln_in_b, attn_w_in, attn_w_out, fourier_w_out, router_w, router_b,
          expert_w_gate, expert_w_up, expert_w_down, ln_g, ln_b):
    S = x.shape[1]
    cos, sin = rotary_tables(S)
    h = layer_norm(x, ln_in_g, ln_in_b)
    ia, jf = 0, 0
    for i in range(DEPTH):
        if i % N_MIXERS == 0:
            mix = dilated_mixture_attention(h, attn_w_in[ia], attn_w_out[ia], cos, sin)
            ia += 1
        else:
            mix = fourier_mixer(h, fourier_w_out[jf])
            jf += 1
        h = layer_norm(ALPHA * h + mix, ln_g[i, 0], ln_b[i, 0])
        f = grouped_moe(h, router_w, router_b, expert_w_gate[i], expert_w_up[i], expert_w_down[i])
        h = layer_norm(ALPHA * h + f, ln_g[i, 1], ln_b[i, 1])
    return h


def setup_inputs(seed: int = 0) -> dict:
    key = jax.random.key(seed)
    ks = jax.random.split(key, 16)
    f32 = jnp.float32
    nrm = lambda k, shape, scale: jax.random.normal(k, shape, f32) * scale
    return {
        "x_prompt": nrm(ks[0], (BATCH, SEQ, D_MODEL), 1.0),
        "x_sample": nrm(ks[1], (DEC_BATCH, DEC_SEQ, D_MODEL), 1.0),
        "ln_in_g": 1.0 + nrm(ks[2], (D_MODEL,), 0.02),
        "ln_in_b": nrm(ks[3], (D_MODEL,), 0.02),
        "attn_w_in": nrm(ks[4], (N_ATTN_LAYERS, D_MODEL, N_DIL_GROUPS * 3 * ATTN_WIDTH), D_MODEL ** -0.5),
        "attn_w_out": nrm(ks[5], (N_ATTN_LAYERS, ATTN_WIDTH, D_MODEL), BETA * ATTN_WIDTH ** -0.5),
        "fourier_w_out": nrm(ks[6], (N_FOURIER_LAYERS, D_MODEL, D_MODEL), BETA * D_MODEL ** -0.5),
        "router_w": nrm(ks[7], (D_MODEL, N_EXPERTS), D_MODEL ** -0.5),
        "router_b": nrm(ks[8], (N_EXPERTS,), 0.01),
        "expert_w_gate": nrm(ks[9], (DEPTH, N_EXPERTS, D_MODEL, D_EXPERT), D_MODEL ** -0.5),
        "expert_w_up": nrm(ks[10], (DEPTH, N_EXPERTS, D_MODEL, D_EXPERT), D_MODEL ** -0.5),
        "expert_w_down": nrm(ks[11], (DEPTH, N_EXPERTS, D_EXPERT, D_MODEL), BETA * D_EXPERT ** -0.5),
        "ln_g": 1.0 + nrm(ks[12], (DEPTH, 2, D_MODEL), 0.02),
        "ln_b": nrm(ks[13], (DEPTH, 2, D_MODEL), 0.02),
    }


def reference(x_prompt, x_sample, ln_in_g, ln_in_b, attn_w_in, attn_w_out, fourier_w_out,
              router_w, router_b, expert_w_gate, expert_w_up, expert_w_down, ln_g, ln_b):
    y_prompt = trunk(x_prompt, ln_in_g, ln_in_b, attn_w_in, attn_w_out, fourier_w_out, router_w,
                     router_b, expert_w_gate, expert_w_up, expert_w_down, ln_g, ln_b)
    y_sample = trunk(x_sample, ln_in_g, ln_in_b, attn_w_in, attn_w_out, fourier_w_out, router_w,
                     router_b, expert_w_gate, expert_w_up, expert_w_down, ln_g, ln_b)
    return (y_prompt, y_sample)
```

```python
import functools
import math

import numpy as np
import jax
import jax.numpy as jnp
from jax import lax
from jax.experimental import pallas as pl
from jax.experimental.pallas import tpu as pltpu

F32 = jnp.float32
BF16 = jnp.bfloat16

DILATION_PATTERNS = ((128, 1), (512, 4), (2048, 16))
N_DIL_GROUPS = len(DILATION_PATTERNS)
HEAD_DIM = 128
HEADS_PER_GROUP = 8
ATTN_WIDTH = HEADS_PER_GROUP * HEAD_DIM
ROT_DIM = HEAD_DIM // 4
ROT_HALF = ROT_DIM // 2
ROPE_THETA = 500000.0
FOURIER_GROUPS = 8
EXPERTS_PER_GROUP = 8
LN_EPS = 1e-5
NEG_BIG = -1e30

LANES = 128
VMEM_LIMIT_BYTES = 56 * 1024 * 1024

QKV_TM = 1024
QKV_TN = ATTN_WIDTH
ATT_BLK = 64
TOK_TM = 256
EXP_TM = 256
SEQ_N2 = 128

R_E1, R_E2, R_RANK1, R_RANK2, R_G1, R_G2 = 0, 1, 2, 3, 4, 5


def _cparams(sem, vmem=VMEM_LIMIT_BYTES):
    return pltpu.CompilerParams(dimension_semantics=sem, vmem_limit_bytes=vmem)


def _layer_norm_rows(x, g, b):
    mu = jnp.mean(x, axis=-1, keepdims=True)
    xc = x - mu
    var = jnp.mean(xc * xc, axis=-1, keepdims=True)
    return xc * lax.rsqrt(var + LN_EPS) * g + b


def _ln_qkv_kernel(x_ref, g_ref, b_ref, w_ref, c_ref, s1_ref, s2_ref, qkv_ref, xn_ref, perm_ref, *, q_scale):
    j = pl.program_id(1)

    @pl.when(j == 0)
    def _():
        xn_ref[...] = _layer_norm_rows(x_ref[...], g_ref[...], b_ref[...]).astype(BF16)

    acc = jnp.dot(xn_ref[...], w_ref[...], preferred_element_type=F32)
    tm, tn = acc.shape
    reps = tn // LANES

    def rotary(a):
        c = jnp.tile(c_ref[...], (1, reps))
        s1 = jnp.tile(s1_ref[...], (1, reps))
        s2 = jnp.tile(s2_ref[...], (1, reps))
        return a * c + pltpu.roll(a, tn - ROT_HALF, 1) * s1 + pltpu.roll(a, ROT_HALF, 1) * s2

    for g, (_, dil) in enumerate(DILATION_PATTERNS):
        for c in range(3):
            @pl.when(j == g * 3 + c)
            def _(c=c, dil=dil):
                if c == 0:
                    val = rotary(acc) * q_scale
                elif c == 1:
                    val = rotary(acc)
                else:
                    val = acc
                if dil == 1:
                    qkv_ref[...] = val.astype(BF16)
                else:
                    chunk = tm // dil
                    for cb in range(reps):
                        ls = slice(cb * LANES, (cb + 1) * LANES)
                        perm_ref[cb] = val[:, ls]
                        for r in range(dil):
                            qkv_ref[r * chunk:(r + 1) * chunk, ls] = (
                                perm_ref[cb, pl.ds(r, chunk, stride=dil), :].astype(BF16))


def _ln_qkv(x, g, b, w_bf16, cos_t, sin1_t, sin2_t):
    T, D = x.shape
    N = w_bf16.shape[1]
    tm, tn = QKV_TM, QKV_TN
    assert T % tm == 0 and N == N_DIL_GROUPS * 3 * tn
    kern = functools.partial(_ln_qkv_kernel, q_scale=HEAD_DIM ** -0.5)
    return pl.pallas_call(
        kern,
        out_shape=jax.ShapeDtypeStruct((T, N), BF16),
        grid=(T // tm, N // tn),
        in_specs=[
            pl.BlockSpec((tm, D), lambda i, j: (i, 0)),
            pl.BlockSpec((1, D), lambda i, j: (0, 0)),
            pl.BlockSpec((1, D), lambda i, j: (0, 0)),
            pl.BlockSpec((D, tn), lambda i, j: (0, j)),
            pl.BlockSpec((tm, LANES), lambda i, j: (i, 0)),
            pl.BlockSpec((tm, LANES), lambda i, j: (i, 0)),
            pl.BlockSpec((tm, LANES), lambda i, j: (i, 0)),
        ],
        out_specs=pl.BlockSpec((tm, tn), lambda i, j: (i, j)),
        scratch_shapes=[pltpu.VMEM((tm, D), BF16), pltpu.VMEM((tn // LANES, tm, LANES), F32)],
        compiler_params=_cparams(("arbitrary", "arbitrary")),
        name="ln_qkv",
    )(x, g, b, w_bf16, cos_t, sin1_t, sin2_t)


def _attn_kernel(qb_ref, k0_ref, k1_ref, k2_ref, ob_ref, rr_ref, nn_ref, nb_ref,
                 q_ref, ka_ref, kb_ref, kc_ref, va_ref, vb_ref, vc_ref, o_ref, l_ref, *, dil):
    u = pl.program_id(0)
    n = nn_ref[u]
    nb = nb_ref[u]
    r = rr_ref[u]
    blk = ATT_BLK
    row = lax.broadcasted_iota(jnp.int32, (blk, 3 * blk), 0)
    col = lax.broadcasted_iota(jnp.int32, (blk, 3 * blk), 1)
    rel = col - row
    mask = (rel >= 0) & (rel <= 2 * blk) & (col >= blk * (1 - n)) & (col < blk * (nb - n + 1))
    lane = lax.broadcasted_iota(jnp.int32, (blk, LANES), 1)
    lse_tile = jnp.zeros((blk, LANES), F32)
    for h in range(HEADS_PER_GROUP):
        hs = slice(h * HEAD_DIM, (h + 1) * HEAD_DIM)
        qh = q_ref[:, hs]
        kh = jnp.concatenate([ka_ref[:, hs], kb_ref[:, hs], kc_ref[:, hs]], axis=0)
        vh = jnp.concatenate([va_ref[:, hs], vb_ref[:, hs], vc_ref[:, hs]], axis=0)
        s = lax.dot_general(qh, kh, (((1,), (1,)), ((), ())), preferred_element_type=F32)
        s = jnp.where(mask, s, NEG_BIG)
        m = jnp.max(s, axis=-1, keepdims=True)
        p = jnp.exp(s - m)
        den = jnp.sum(p, axis=-1, keepdims=True)
        o = jnp.dot(p.astype(BF16), vh, preferred_element_type=F32) / den
        if dil == 1:
            o_ref[h] = o
        else:
            o_ref[h, pl.ds(r, blk, stride=dil), :] = o
        lse_tile = jnp.where(lane == h, m + jnp.log(den), lse_tile)
    if dil == 1:
        l_ref[...] = lse_tile
    else:
        l_ref[pl.ds(r, blk, stride=dil), :] = lse_tile


def _attn_units(trunks, dil):
    chunk = QKV_TM // dil
    per_tile = QKV_TM // ATT_BLK
    cols = [[] for _ in range(8)]
    for base, B, S in trunks:
        assert base % QKV_TM == 0 and S % QKV_TM == 0 and S % (ATT_BLK * dil) == 0
        L = S // dil
        nb = L // ATT_BLK
        for b in range(B):
            t0 = (base + b * S) // QKV_TM
            for n in range(nb):
                for r in range(dil):
                    def blk64(nn):
                        nn = min(max(nn, 0), nb - 1)
                        tt = t0 + (ATT_BLK * nn * dil) // QKV_TM
                        off = (ATT_BLK * nn) % chunk
                        return tt * per_tile + (r * chunk + off) // ATT_BLK
                    vals = (blk64(n), blk64(n - 1), blk64(n), blk64(n + 1),
                            (base + b * S) // (ATT_BLK * dil) + n, r, n, nb)
                    for cl, v in zip(cols, vals):
                        cl.append(v)
    return [np.asarray(cl, np.int32) for cl in cols]


def _attention_group(qkv, g, dil, trunks):
    T = qkv.shape[0]
    meta = _attn_units(trunks, dil)
    n_units = len(meta[0])
    blk = ATT_BLK

    def qspec(which, c):
        return pl.BlockSpec((blk, ATTN_WIDTH), lambda u, *refs: (refs[which][u], g * 3 + c))

    kern = functools.partial(_attn_kernel, dil=dil)
    grid_spec = pltpu.PrefetchScalarGridSpec(
        num_scalar_prefetch=8,
        grid=(n_units,),
        in_specs=[qspec(0, 0), qspec(1, 1), qspec(2, 1), qspec(3, 1), qspec(1, 2), qspec(2, 2), qspec(3, 2)],
        out_specs=(
            pl.BlockSpec((HEADS_PER_GROUP, blk * dil, HEAD_DIM), lambda u, *refs: (0, refs[4][u], 0)),
            pl.BlockSpec((blk * dil, LANES), lambda u, *refs: (refs[4][u], 0)),
        ),
    )
    return pl.pallas_call(
        kern,
        out_shape=(jax.ShapeDtypeStruct((HEADS_PER_GROUP, T, HEAD_DIM), F32),
                   jax.ShapeDtypeStruct((T, LANES), F32)),
        grid_spec=grid_spec,
        compiler_params=_cparams(("arbitrary",)),
        name=f"band_attn_d{dil}",
    )(*[jnp.asarray(m) for m in meta], qkv, qkv, qkv, qkv, qkv, qkv, qkv)


def _combine_groups(o_refs, l_refs):
    ls = [l[...] for l in l_refs]
    m = functools.reduce(jnp.maximum, ls)
    es = [jnp.exp(l - m) for l in ls]
    inv = 1.0 / functools.reduce(lambda a, b: a + b, es)
    ws = [e * inv for e in es]
    parts = []
    for h in range(HEADS_PER_GROUP):
        acc = None
        for w, o in zip(ws, o_refs):
            term = w[:, h:h + 1] * o[h]
            acc = term if acc is None else acc + term
        parts.append(acc)
    return jnp.concatenate(parts, axis=1).astype(BF16)


def _route(h1, rw_ref, rb_ref, cnt_ref, route_ref, n_experts):
    tm = h1.shape[0]
    a_hi = h1.astype(BF16)
    a_lo = (h1 - a_hi.astype(F32)).astype(BF16)
    r1 = jnp.dot(a_hi, rw_ref[...], preferred_element_type=F32)
    r2 = jnp.dot(a_lo, rw_ref[...], preferred_element_type=F32)
    tot = r1 + r2
    logits = tot + pltpu.roll(tot, LANES - n_experts, 1) + rb_ref[...]
    lane = lax.broadcasted_iota(jnp.int32, (tm, LANES), 1)
    lg = jnp.where(lane < n_experts, logits, -jnp.inf)
    m1 = jnp.max(lg, axis=-1, keepdims=True)
    i1 = jnp.min(jnp.where(lg == m1, lane, LANES), axis=-1, keepdims=True)
    in_grp = ((lane // EXPERTS_PER_GROUP) == (i1 // EXPERTS_PER_GROUP)) & (lane != i1)
    lg2 = jnp.where(in_grp, lg, -jnp.inf)
    m2 = jnp.max(lg2, axis=-1, keepdims=True)
    i2 = jnp.min(jnp.where(lg2 == m2, lane, LANES), axis=-1, keepdims=True)
    e2 = jnp.exp(m2 - m1)
    g1 = 1.0 / (1.0 + e2)
    g2 = e2 / (1.0 + e2)
    oh1 = lane == i1
    oh2 = lane == i2
    ohs = jnp.where(oh1 | oh2, 1.0, 0.0)
    rr = lax.broadcasted_iota(jnp.int32, (tm, tm), 0)
    cc = lax.broadcasted_iota(jnp.int32, (tm, tm), 1)
    tri = jnp.where(cc < rr, 1.0, 0.0).astype(BF16)
    before = jnp.dot(tri, ohs.astype(BF16), preferred_element_type=F32) + cnt_ref[0:1, :]
    rank1 = jnp.sum(jnp.where(oh1, before, 0.0), axis=-1, keepdims=True)
    rank2 = jnp.sum(jnp.where(oh2, before, 0.0), axis=-1, keepdims=True)
    cnt_ref[...] = cnt_ref[...] + jnp.sum(ohs, axis=0, keepdims=True)
    rec = jnp.zeros((tm, LANES), F32)
    for ln, v in ((R_E1, i1.astype(F32)), (R_E2, i2.astype(F32)), (R_RANK1, rank1), (R_RANK2, rank2),
                  (R_G1, g1), (R_G2, g2)):
        rec = jnp.where(lane == ln, v, rec)
    route_ref[...] = rec


def _tail_kernel(*refs, n_mix, alpha, n_experts):
    mix_refs = refs[:n_mix]
    (hres_ref, pg_ref, pb_ref, w_ref, g_ref, b_ref, rw_ref, rb_ref, h_ref, route_ref, cnt_ref) = refs[n_mix:]

    @pl.when(pl.program_id(0) == 0)
    def _():
        cnt_ref[...] = jnp.zeros_like(cnt_ref)

    if n_mix == 1:
        mixin = mix_refs[0][...]
        hres = hres_ref[...]
    else:
        mixin = _combine_groups(mix_refs[:N_DIL_GROUPS], mix_refs[N_DIL_GROUPS:])
        hres = _layer_norm_rows(hres_ref[...], pg_ref[...], pb_ref[...])
    mix = jnp.dot(mixin, w_ref[...], preferred_element_type=F32)
    h1 = _layer_norm_rows(alpha * hres + mix, g_ref[...], b_ref[...])
    h_ref[...] = h1
    _route(h1, rw_ref, rb_ref, cnt_ref, route_ref, n_experts)


def _mixer_tail(mix_inputs, hres, pre_g, pre_b, w_bf16, g, b, rw_cat, rb_pad, alpha, n_experts):
    T, D = hres.shape
    tm = TOK_TM
    assert T % tm == 0
    kw = w_bf16.shape[0]
    n_mix = len(mix_inputs)
    mix_specs = [pl.BlockSpec((tm, a.shape[1]), lambda i: (i, 0)) if a.ndim == 2
                 else pl.BlockSpec((a.shape[0], tm, a.shape[2]), lambda i: (0, i, 0)) for a in mix_inputs]
    kern = functools.partial(_tail_kernel, n_mix=n_mix, alpha=alpha, n_experts=n_experts)
    return pl.pallas_call(
        kern,
        out_shape=(jax.ShapeDtypeStruct((T, D), F32), jax.ShapeDtypeStruct((T, LANES), F32),
                   jax.ShapeDtypeStruct((8, LANES), F32)),
        grid=(T // tm,),
        in_specs=mix_specs + [
            pl.BlockSpec((tm, D), lambda i: (i, 0)),
            pl.BlockSpec((1, D), lambda i: (0, 0)),
            pl.BlockSpec((1, D), lambda i: (0, 0)),
            pl.BlockSpec((kw, D), lambda i: (0, 0)),
            pl.BlockSpec((1, D), lambda i: (0, 0)),
            pl.BlockSpec((1, D), lambda i: (0, 0)),
            pl.BlockSpec((D, LANES), lambda i: (0, 0)),
            pl.BlockSpec((1, LANES), lambda i: (0, 0)),
        ],
        out_specs=(
            pl.BlockSpec((tm, D), lambda i: (i, 0)),
            pl.BlockSpec((tm, LANES), lambda i: (i, 0)),
            pl.BlockSpec((8, LANES), lambda i: (0, 0)),
        ),
        compiler_params=_cparams(("arbitrary",)),
        name="mixer_tail_router",
    )(*mix_inputs, hres, pre_g, pre_b, w_bf16, g, b, rw_cat, rb_pad)


def _row_copy(src_ref, src_row, dst_ref, dst_row, sem):
    return pltpu.make_async_copy(src_ref.at[pl.ds(src_row, 1)], dst_ref.at[pl.ds(dst_row, 1)], sem)


def _dispatch_kernel(dest_ref, h_ref, xs_in_ref, xs_ref, sem):
    del xs_in_ref
    tm = h_ref.shape[0]

    def issue(r, carry):
        _row_copy(h_ref, r, xs_ref, dest_ref[0, 0, r], sem).start()
        _row_copy(h_ref, r, xs_ref, dest_ref[0, 0, tm + r], sem).start()
        return carry

    lax.fori_loop(0, tm, issue, 0)

    def drain(r, carry):
        _row_copy(h_ref, 0, xs_ref, 0, sem).wait()
        _row_copy(h_ref, 0, xs_ref, 0, sem).wait()
        return carry

    lax.fori_loop(0, tm, drain, 0)


def _dispatch(h, dest_tiles, n_rows):
    T, D = h.shape
    tm = TOK_TM
    xs0 = jnp.zeros((n_rows, D), F32)
    return pl.pallas_call(
        _dispatch_kernel,
        out_shape=jax.ShapeDtypeStruct((n_rows, D), F32),
        grid=(T // tm,),
        in_specs=[
            pl.BlockSpec((1, 1, 2 * tm), lambda i: (i, 0, 0), memory_space=pltpu.SMEM),
            pl.BlockSpec((tm, D), lambda i: (i, 0)),
            pl.BlockSpec(memory_space=pl.ANY),
        ],
        out_specs=pl.BlockSpec(memory_space=pl.ANY),
        scratch_shapes=[pltpu.SemaphoreType.DMA(())],
        input_output_aliases={2: 0},
        compiler_params=_cparams(("arbitrary",)),
        name="expert_dispatch",
    )(dest_tiles, h, xs0)


def _gmm_kernel(te_ref, nv_ref, *refs, mode):
    if mode == "swiglu":
        x_ref, w_ref, a_ref, o_ref, wbf_ref = refs
    else:
        x_ref, w_ref, o_ref, wbf_ref = refs
    j = pl.program_id(0)

    @pl.when(j < nv_ref[0])
    def _():
        prev = te_ref[jnp.maximum(j - 1, 0)]

        @pl.when((j == 0) | (te_ref[j] != prev))
        def _():
            wbf_ref[...] = w_ref[0, 0].astype(BF16)

        acc = jnp.dot(x_ref[...].astype(BF16), wbf_ref[...], preferred_element_type=F32)
        if mode == "swiglu":
            a = a_ref[...].astype(F32)
            acc = a * (1.0 / (1.0 + jnp.exp(-a))) * acc
        o_ref[...] = acc.astype(o_ref.dtype)

    @pl.when(j >= nv_ref[0])
    def _():
        o_ref[...] = jnp.zeros_like(o_ref)


def _gmm(x, w, layer, tile_expert, n_valid, out_dtype, a=None):
    P, K = x.shape
    N = w.shape[-1]
    tm = EXP_TM
    n_tiles = P // tm
    mode = "plain" if a is None else "swiglu"

    def row_map(j, te, nv):
        return (jnp.minimum(j, nv[0] - 1), 0)

    def w_map(j, te, nv):
        return (layer, te[jnp.minimum(j, nv[0] - 1)], 0, 0)

    in_specs = [pl.BlockSpec((tm, K), row_map), pl.BlockSpec((1, 1, K, N), w_map)]
    args = [x, w]
    if a is not None:
        in_specs.append(pl.BlockSpec((tm, N), row_map))
        args.append(a)
    grid_spec = pltpu.PrefetchScalarGridSpec(
        num_scalar_prefetch=2,
        grid=(n_tiles,),
        in_specs=in_specs,
        out_specs=pl.BlockSpec((tm, N), lambda j, te, nv: (j, 0)),
        scratch_shapes=[pltpu.VMEM((K, N), BF16)],
    )
    return pl.pallas_call(
        functools.partial(_gmm_kernel, mode=mode),
        out_shape=jax.ShapeDtypeStruct((P, N), out_dtype),
        grid_spec=grid_spec,
        compiler_params=_cparams(("arbitrary",)),
        name=f"expert_matmul_{mode}",
    )(tile_expert, n_valid, *args)


def _combine_kernel(dest_ref, route_ref, hres_ref, y_ref, g_ref, b_ref, h_ref, hb_ref, buf_ref, sem, *, alpha):
    tm = hres_ref.shape[0]

    def issue(r, carry):
        _row_copy(y_ref, dest_ref[0, 0, r], buf_ref.at[0], r, sem).start()
        _row_copy(y_ref, dest_ref[0, 0, tm + r], buf_ref.at[1], r, sem).start()
        return carry

    lax.fori_loop(0, tm, issue, 0)

    def drain(r, carry):
        _row_copy(y_ref, 0, buf_ref.at[0], 0, sem).wait()
        _row_copy(y_ref, 0, buf_ref.at[1], 0, sem).wait()
        return carry

    lax.fori_loop(0, tm, drain, 0)
    rec = route_ref[...]
    f = rec[:, R_G1:R_G1 + 1] * buf_ref[0] + rec[:, R_G2:R_G2 + 1] * buf_ref[1]
    h2 = _layer_norm_rows(alpha * hres_ref[...] + f, g_ref[...], b_ref[...])
    h_ref[...] = h2
    hb_ref[...] = h2.astype(BF16)


def _combine(dest_tiles, route, hres, y, g, b, alpha):
    T, D = hres.shape
    tm = TOK_TM
    return pl.pallas_call(
        functools.partial(_combine_kernel, alpha=alpha),
        out_shape=(jax.ShapeDtypeStruct((T, D), F32), jax.ShapeDtypeStruct((T, D), BF16)),
        grid=(T // tm,),
        in_specs=[
            pl.BlockSpec((1, 1, 2 * tm), lambda i: (i, 0, 0), memory_space=pltpu.SMEM),
            pl.BlockSpec((tm, LANES), lambda i: (i, 0)),
            pl.BlockSpec((tm, D), lambda i: (i, 0)),
            pl.BlockSpec(memory_space=pl.ANY),
            pl.BlockSpec((1, D), lambda i: (0, 0)),
            pl.BlockSpec((1, D), lambda i: (0, 0)),
        ],
        out_specs=(
            pl.BlockSpec((tm, D), lambda i: (i, 0)),
            pl.BlockSpec((tm, D), lambda i: (i, 0)),
        ),
        scratch_shapes=[pltpu.VMEM((2, tm, D), F32), pltpu.SemaphoreType.DMA(())],
        compiler_params=_cparams(("arbitrary",)),
        name="expert_combine_ln",
    )(dest_tiles, route, hres, y, g, b)


def _grouped_moe(h, route, counts, w_gate, w_up, w_down, layer, g, b, alpha):
    T, D = h.shape
    E = w_gate.shape[1]
    tm = EXP_TM
    n_tiles = (T * 2) // tm + E
    cnt = counts[0, :E].astype(jnp.int32)
    padded = ((cnt + tm - 1) // tm) * tm
    pad_end = jnp.cumsum(padded)
    pad_start = pad_end - padded
    e1 = route[:, R_E1].astype(jnp.int32)
    e2 = route[:, R_E2].astype(jnp.int32)
    d1 = pad_start[e1] + route[:, R_RANK1].astype(jnp.int32)
    d2 = pad_start[e2] + route[:, R_RANK2].astype(jnp.int32)
    dest_tiles = jnp.concatenate([d1.reshape(T // TOK_TM, 1, TOK_TM), d2.reshape(T // TOK_TM, 1, TOK_TM)], axis=2)
    tile_expert = jnp.searchsorted(pad_end, jnp.arange(n_tiles, dtype=jnp.int32) * tm, side="right")
    tile_expert = jnp.minimum(tile_expert, E - 1).astype(jnp.int32)
    n_valid = (pad_end[-1:] // tm).astype(jnp.int32)

    xs = _dispatch(h, dest_tiles, n_tiles * tm)
    a = _gmm(xs, w_gate, layer, tile_expert, n_valid, BF16)
    hid = _gmm(xs, w_up, layer, tile_expert, n_valid, BF16, a=a)
    y = _gmm(hid, w_down, layer, tile_expert, n_valid, F32)
    return _combine(dest_tiles, route, h, y, g, b, alpha)


def _fourier_a_kernel(x_ref, cs_ref, f_ref, yr_ref, yi_ref, *, gd):
    n1 = x_ref.shape[0]
    zr, zi = [], []
    for gi in range(FOURIER_GROUPS):
        pq = jnp.dot(x_ref[:, gi * gd:(gi + 1) * gd], cs_ref[...], preferred_element_type=F32)
        zr.append(pq[:, :gd])
        zi.append(pq[:, gd:])
    z = jnp.concatenate([jnp.concatenate(zr, axis=1), jnp.concatenate(zi, axis=1)], axis=0).astype(BF16)
    y = jnp.dot(f_ref[0], z, preferred_element_type=F32)
    yr_ref[...] = y[:n1].astype(BF16)
    yi_ref[...] = y[n1:].astype(BF16)


def _fourier_b_kernel(yr_ref, yi_ref, f_ref, o_ref, *, norm):
    y = jnp.concatenate([yr_ref[...], yi_ref[...]], axis=0)
    o_ref[...] = (jnp.dot(f_ref[...], y, preferred_element_type=F32) * norm).astype(BF16)


def _dft_tables(S, gd):
    n1, n2 = S // SEQ_N2, SEQ_N2
    jc = (jnp.arange(gd)[:, None] * jnp.arange(gd)[None, :]) % gd
    ang = jc.astype(F32) * (2.0 * math.pi / gd)
    cs = jnp.concatenate([jnp.cos(ang), -jnp.sin(ang)], axis=1).astype(BF16)
    k1 = jnp.arange(n1)[None, :, None]
    s1 = jnp.arange(n1)[None, None, :]
    s2 = jnp.arange(n2)[:, None, None]
    th = ((k1 * (n2 * s1 + s2)) % S).astype(F32) * (2.0 * math.pi / S)
    fr, fi = jnp.cos(th), -jnp.sin(th)
    f1 = jnp.concatenate([jnp.concatenate([fr, -fi], axis=2), jnp.concatenate([fi, fr], axis=2)], axis=1).astype(BF16)
    kk = (jnp.arange(n2)[:, None] * jnp.arange(n2)[None, :]) % n2
    th2 = kk.astype(F32) * (2.0 * math.pi / n2)
    f2 = jnp.concatenate([jnp.cos(th2), jnp.sin(th2)], axis=1).astype(BF16)
    return cs, f1, f2


def _fourier_mixed(hb, B, S):
    Tt, D = hb.shape
    gd = D // FOURIER_GROUPS
    n1, n2 = S // SEQ_N2, SEQ_N2
    cs, f1, f2 = _dft_tables(S, gd)
    xv = hb.reshape(B * n1, n2 * D)
    yr, yi = pl.pallas_call(
        functools.partial(_fourier_a_kernel, gd=gd),
        out_shape=(jax.ShapeDtypeStruct((B * n1, n2 * D), BF16),) * 2,
        grid=(B, n2),
        in_specs=[
            pl.BlockSpec((n1, D), lambda b, s: (b, s)),
            pl.BlockSpec((gd, 2 * gd), lambda b, s: (0, 0)),
            pl.BlockSpec((1, 2 * n1, 2 * n1), lambda b, s: (s, 0, 0)),
        ],
        out_specs=(pl.BlockSpec((n1, D), lambda b, s: (b, s)),) * 2,
        compiler_params=_cparams(("arbitrary", "arbitrary")),
        name="fourier_channel_step1",
    )(xv, cs, f1)
    yr = yr.reshape(Tt, D)
    yi = yi.reshape(Tt, D)
    mixed = pl.pallas_call(
        functools.partial(_fourier_b_kernel, norm=1.0 / math.sqrt(S * gd)),
        out_shape=jax.ShapeDtypeStruct((B * n2, n1 * D), BF16),
        grid=(B, n1),
        in_specs=[
            pl.BlockSpec((n2, D), lambda b, k: (b * n1 + k, 0)),
            pl.BlockSpec((n2, D), lambda b, k: (b * n1 + k, 0)),
            pl.BlockSpec((n2, 2 * n2), lambda b, k: (0, 0)),
        ],
        out_specs=pl.BlockSpec((n2, D), lambda b, k: (b, k)),
        compiler_params=_cparams(("arbitrary", "arbitrary")),
        name="fourier_step2",
    )(yr, yi, f2)
    return mixed.reshape(Tt, D)


def _rotary_tables(trunks, T):
    inv_freq = ROPE_THETA ** (-jnp.arange(0, ROT_DIM, 2, dtype=F32) / ROT_DIM)
    pos = jnp.concatenate([jnp.tile(jnp.arange(S, dtype=F32), B) for _, B, S in trunks])
    ang = pos[:, None] * inv_freq[None, :]
    cos, sin = jnp.cos(ang), jnp.sin(ang)
    zeros = jnp.zeros((T, LANES - ROT_DIM), F32)
    z16 = jnp.zeros((T, ROT_HALF), F32)
    cos_t = jnp.concatenate([cos, cos, jnp.ones((T, LANES - ROT_DIM), F32)], axis=1)
    sin1_t = jnp.concatenate([-sin, z16, zeros], axis=1)
    sin2_t = jnp.concatenate([z16, sin, zeros], axis=1)
    return cos_t, sin1_t, sin2_t


def kernel(x_prompt, x_sample, ln_in_g, ln_in_b, attn_w_in, attn_w_out, fourier_w_out, router_w, router_b,
           expert_w_gate, expert_w_up, expert_w_down, ln_g, ln_b):
    Bp, Sp, D = x_prompt.shape
    Bs, Ss, _ = x_sample.shape
    depth = ln_g.shape[0]
    E = router_w.shape[1]
    alpha = (2 * depth) ** 0.25
    trunks = ((0, Bp, Sp), (Bp * Sp, Bs, Ss))
    T = Bp * Sp + Bs * Ss
    assert 2 * E == LANES and T % QKV_TM == 0

    x = jnp.concatenate([x_prompt.reshape(Bp * Sp, D), x_sample.reshape(Bs * Ss, D)], axis=0)
    row = lambda v: v.reshape(1, -1).astype(F32)
    rw_hi = router_w.astype(BF16)
    rw_lo = (router_w - rw_hi.astype(F32)).astype(BF16)
    rw_cat = jnp.concatenate([rw_hi, rw_lo], axis=1)
    rb_pad = jnp.concatenate([router_b.astype(F32), jnp.zeros((LANES - E,), F32)]).reshape(1, LANES)
    cos_t, sin1_t, sin2_t = _rotary_tables(trunks, T)

    assert depth == 2, "layer 0 (attention) is fused with the input LayerNorm, layer 1 is the Fourier layer"
    gin, bin_ = row(ln_in_g), row(ln_in_b)

    qkv = _ln_qkv(x, gin, bin_, attn_w_in[0].astype(BF16), cos_t, sin1_t, sin2_t)
    outs, lses = [], []
    for g, (_, dil) in enumerate(DILATION_PATTERNS):
        o, l = _attention_group(qkv, g, dil, trunks)
        outs.append(o)
        lses.append(l)
    h, route, counts = _mixer_tail(outs + lses, x, gin, bin_, attn_w_out[0].astype(BF16),
                                   row(ln_g[0, 0]), row(ln_b[0, 0]), rw_cat, rb_pad, alpha, E)
    h, hb = _grouped_moe(h, route, counts, expert_w_gate, expert_w_up, expert_w_down, 0,
                         row(ln_g[0, 1]), row(ln_b[0, 1]), alpha)

    parts = [_fourier_mixed(lax.slice_in_dim(hb, base, base + B * S, axis=0), B, S) for base, B, S in trunks]
    mixed = jnp.concatenate(parts, axis=0)
    h, route, counts = _mixer_tail([mixed], h, gin, bin_, fourier_w_out[0].astype(BF16),
                                   row(ln_g[1, 0]), row(ln_b[1, 0]), rw_cat, rb_pad, alpha, E)
    h, hb = _grouped_moe(h, route, counts, expert_w_gate, expert_w_up, expert_w_down, 1,
                         row(ln_g[1, 1]), row(ln_b[1, 1]), alpha)
    y_prompt = h[:Bp * Sp].reshape(Bp, Sp, D)
    y_sample = h[Bp * Sp:].reshape(Bs, Ss, D)
    return (y_prompt, y_sample)
```

```python
import functools
import math

import numpy as np
import jax
import jax.numpy as jnp
from jax import lax
from jax.experimental import pallas as pl
from jax.experimental.pallas import tpu as pltpu

F32 = jnp.float32
BF16 = jnp.bfloat16
U32 = jnp.uint32

DILATION_PATTERNS = ((128, 1), (512, 4), (2048, 16))
N_DIL_GROUPS = len(DILATION_PATTERNS)
HEAD_DIM = 128
HEADS_PER_GROUP = 8
ATTN_WIDTH = HEADS_PER_GROUP * HEAD_DIM
ROT_DIM = HEAD_DIM // 4
ROT_HALF = ROT_DIM // 2
ROT_PARTNER = HEAD_DIM // 2
ROPE_THETA = 500000.0
FOURIER_GROUPS = 8
EXPERTS_PER_GROUP = 8
LN_EPS = 1e-5
NEG_BIG = -1e30

LANES = 128
SUBLANES = 8
VMEM_LIMIT_BYTES = 56 * 1024 * 1024

QKV_TM = 1024
QKV_TN = ATTN_WIDTH
ATT_BLK = 64
ATT_SUB = 4
TOK_TM = 256
EXP_TM = 512
SEQ_N2 = 128

R_E1, R_E2, R_RANK1, R_RANK2, R_G1, R_G2 = 0, 1, 2, 3, 4, 5
PLAN_NV_ROW = SUBLANES - 1


def _cparams(sem, vmem=VMEM_LIMIT_BYTES):
    return pltpu.CompilerParams(dimension_semantics=sem, vmem_limit_bytes=vmem)


def _layer_norm_rows(x, g, b):
    mu = jnp.mean(x, axis=-1, keepdims=True)
    xc = x - mu
    var = jnp.mean(xc * xc, axis=-1, keepdims=True)
    return xc * lax.rsqrt(var + LN_EPS) * g + b


def _ln_qkv_kernel(x_ref, g_ref, b_ref, w_ref, c_ref, s_ref, qkv_ref, xn_ref, perm_ref, *, q_scale):
    j = pl.program_id(1)

    @pl.when(j == 0)
    def _():
        xn_ref[...] = _layer_norm_rows(x_ref[...], g_ref[...], b_ref[...]).astype(BF16)

    acc = jnp.dot(xn_ref[...], w_ref[...], preferred_element_type=F32)
    tm, tn = acc.shape
    reps = tn // LANES

    for g, (_, dil) in enumerate(DILATION_PATTERNS):
        for c in range(3):
            @pl.when(j == g * 3 + c)
            def _(c=c, dil=dil):
                chunk = tm // dil
                for cb in range(reps):
                    ls = slice(cb * LANES, (cb + 1) * LANES)
                    val = acc[:, ls]
                    if c < 2:
                        val = val * c_ref[...] + pltpu.roll(val, ROT_PARTNER, 1) * s_ref[...]
                    if c == 0:
                        val = val * q_scale
                    if dil == 1:
                        qkv_ref[:, ls] = val.astype(BF16)
                    else:
                        perm_ref[cb] = val
                        for r in range(dil):
                            qkv_ref[r * chunk:(r + 1) * chunk, ls] = (
                                perm_ref[cb, pl.ds(r, chunk, stride=dil), :].astype(BF16))


def _ln_qkv(x, g, b, w_bf16, cos_t, sin_t):
    T, D = x.shape
    N = w_bf16.shape[1]
    tm, tn = QKV_TM, QKV_TN
    assert T % tm == 0 and N == N_DIL_GROUPS * 3 * tn
    kern = functools.partial(_ln_qkv_kernel, q_scale=HEAD_DIM ** -0.5)
    return pl.pallas_call(
        kern,
        out_shape=jax.ShapeDtypeStruct((T, N), BF16),
        grid=(T // tm, N // tn),
        in_specs=[
            pl.BlockSpec((tm, D), lambda i, j: (i, 0)),
            pl.BlockSpec((1, D), lambda i, j: (0, 0)),
            pl.BlockSpec((1, D), lambda i, j: (0, 0)),
            pl.BlockSpec((D, tn), lambda i, j: (0, j)),
            pl.BlockSpec((tm, LANES), lambda i, j: (i, 0)),
            pl.BlockSpec((tm, LANES), lambda i, j: (i, 0)),
        ],
        out_specs=pl.BlockSpec((tm, tn), lambda i, j: (i, j)),
        scratch_shapes=[pltpu.VMEM((tm, D), BF16), pltpu.VMEM((tn // LANES, tm, LANES), F32)],
        compiler_params=_cparams(("arbitrary", "arbitrary")),
        name="ln_qkv",
    )(x, g, b, w_bf16, cos_t, sin_t)


def _attn_kernel(qb_ref, kp_ref, kn_ref, ob_ref, rr_ref, var_ref,
                 q_ref, kc_ref, kp_blk, kn_blk, vc_ref, vp_blk, vn_blk, bias_ref, o_ref, l_ref,
                 *, dil, split_residues):
    u = pl.program_id(0)
    rr = rr_ref[u]
    blk = ATT_BLK
    sb = blk * ATT_SUB
    lane = lax.broadcasted_iota(jnp.int32, (sb, LANES), 1)
    bias = bias_ref[0]
    lse_tile = jnp.zeros((sb, LANES), F32)

    def store(ref, lead, val):
        if dil == 1:
            ref[lead + (slice(None), slice(None))] = val
        elif split_residues:
            for a in range(ATT_SUB):
                dst = pl.ds(rr * ATT_SUB + a, blk, stride=dil)
                ref[lead + (dst, slice(None))] = val[a * blk:(a + 1) * blk]
        else:
            ref[lead + (pl.ds(rr, sb, stride=dil), slice(None))] = val

    for h in range(HEADS_PER_GROUP):
        hs = slice(h * HEAD_DIM, (h + 1) * HEAD_DIM)
        kh = jnp.concatenate([kp_blk[:, hs], kc_ref[:, hs], kn_blk[:, hs]], axis=0)
        vh = jnp.concatenate([vp_blk[:, hs], vc_ref[:, hs], vn_blk[:, hs]], axis=0)
        s = lax.dot_general(q_ref[:, hs], kh, (((1,), (1,)), ((), ())), preferred_element_type=F32) + bias
        m = jnp.max(s, axis=-1, keepdims=True)
        p = jnp.exp(s - m)
        den = jnp.sum(p, axis=-1, keepdims=True)
        o = jnp.dot(p.astype(BF16), vh, preferred_element_type=F32) / den
        lse_tile = jnp.where(lane == h, m + jnp.log(den), lse_tile)
        store(o_ref, (h,), o)
    store(l_ref, (), lse_tile)


def _attn_bias(split_residues):
    blk, sb = ATT_BLK, ATT_BLK * ATT_SUB
    halo = sb if split_residues else blk
    row = np.arange(sb)[:, None]
    col = np.arange(halo + sb + halo)[None, :]
    if split_residues:
        tile, a_k, j_k = col // sb, (col % sb) // blk, col % blk
        ok = (a_k == row // blk) & (np.abs(blk * (tile - 1) + j_k - row % blk) <= blk)
        has_prev, has_next = tile != 0, tile != 2
    else:
        ok = np.abs(col - halo - row) <= blk
        has_prev, has_next = col >= halo, col < halo + sb
    variants = []
    for v in range(4):
        okv = ok & (has_prev | (v & 1 == 0)) & (has_next | (v & 2 == 0))
        variants.append(np.where(okv, 0.0, NEG_BIG))
    return np.stack(variants).astype(np.float32)


def _attn_units(trunks, dil, split_residues):
    sb = ATT_BLK * ATT_SUB
    chunk = QKV_TM // dil
    cols = [[] for _ in range(6)]
    for base, B, S in trunks:
        assert base % QKV_TM == 0 and S % QKV_TM == 0
        L = S // dil
        nb = L // ATT_BLK
        for b in range(B):
            t0 = (base + b * S) // QKV_TM
            if split_residues:
                assert chunk == ATT_BLK and dil % ATT_SUB == 0
                for n in range(nb):
                    for rq in range(dil // ATT_SUB):
                        per = QKV_TM // sb
                        cur = (t0 + n) * per + rq
                        prev = (t0 + max(n - 1, 0)) * per + rq
                        nxt = (t0 + min(n + 1, nb - 1)) * per + rq
                        variant = (n == 0) + 2 * (n == nb - 1)
                        vals = (cur, prev, nxt, (base + b * S) // QKV_TM + n, rq, variant)
                        for cl, v in zip(cols, vals):
                            cl.append(v)
            else:
                assert chunk % sb == 0 and L % sb == 0
                for m in range(L // sb):
                    for r in range(dil):
                        def row_of(n):
                            n = min(max(n, 0), nb - 1)
                            tt = t0 + (ATT_BLK * n * dil) // QKV_TM
                            return tt * QKV_TM + r * chunk + (ATT_BLK * n) % chunk
                        n0 = m * ATT_SUB
                        variant = (n0 == 0) + 2 * (n0 + ATT_SUB == nb)
                        vals = (row_of(n0) // sb, row_of(n0 - 1) // ATT_BLK, row_of(n0 + ATT_SUB) // ATT_BLK,
                                (base + b * S) // (sb * dil) + m, r, variant)
                        for cl, v in zip(cols, vals):
                            cl.append(v)
    return [np.asarray(cl, np.int32) for cl in cols]


def _attention_group(qkv, g, dil, trunks):
    T = qkv.shape[0]
    sb = ATT_BLK * ATT_SUB
    split_residues = (QKV_TM // dil) == ATT_BLK
    meta = _attn_units(trunks, dil, split_residues)
    n_units = len(meta[0])
    halo = sb if split_residues else ATT_BLK
    out_rows = QKV_TM if split_residues else sb * dil

    def spec(rows, which, c):
        return pl.BlockSpec((rows, ATTN_WIDTH), lambda u, *refs: (refs[which][u], g * 3 + c))

    bias = jnp.asarray(_attn_bias(split_residues))
    kern = functools.partial(_attn_kernel, dil=dil, split_residues=split_residues)
    grid_spec = pltpu.PrefetchScalarGridSpec(
        num_scalar_prefetch=6,
        grid=(n_units,),
        in_specs=[spec(sb, 0, 0), spec(sb, 0, 1), spec(halo, 1, 1), spec(halo, 2, 1),
                  spec(sb, 0, 2), spec(halo, 1, 2), spec(halo, 2, 2),
                  pl.BlockSpec((1,) + bias.shape[1:], lambda u, *refs: (refs[5][u], 0, 0))],
        out_specs=(
            pl.BlockSpec((HEADS_PER_GROUP, out_rows, HEAD_DIM), lambda u, *refs: (0, refs[3][u], 0)),
            pl.BlockSpec((out_rows, LANES), lambda u, *refs: (refs[3][u], 0)),
        ),
    )
    return pl.pallas_call(
        kern,
        out_shape=(jax.ShapeDtypeStruct((HEADS_PER_GROUP, T, HEAD_DIM), F32),
                   jax.ShapeDtypeStruct((T, LANES), F32)),
        grid_spec=grid_spec,
        compiler_params=_cparams(("arbitrary",)),
        name=f"band_attn_d{dil}",
    )(*[jnp.asarray(m) for m in meta], qkv, qkv, qkv, qkv, qkv, qkv, qkv, bias)


def _combine_groups(o_refs, l_refs):
    ls = [l[...] for l in l_refs]
    m = functools.reduce(jnp.maximum, ls)
    es = [jnp.exp(l - m) for l in ls]
    inv = 1.0 / functools.reduce(lambda a, b: a + b, es)
    ws = [e * inv for e in es]
    parts = []
    for h in range(HEADS_PER_GROUP):
        acc = None
        for w, o in zip(ws, o_refs):
            term = w[:, h:h + 1] * o[h]
            acc = term if acc is None else acc + term
        parts.append(acc)
    return jnp.concatenate(parts, axis=1).astype(BF16)


def _route(h1, rw_ref, rb_ref, cnt_ref, route_ref, n_experts):
    tm = h1.shape[0]
    a_hi = h1.astype(BF16)
    a_lo = (h1 - a_hi.astype(F32)).astype(BF16)
    r1 = jnp.dot(a_hi, rw_ref[...], preferred_element_type=F32)
    r2 = jnp.dot(a_lo, rw_ref[...], preferred_element_type=F32)
    tot = r1 + r2
    logits = tot + pltpu.roll(tot, LANES - n_experts, 1) + rb_ref[...]
    lane = lax.broadcasted_iota(jnp.int32, (tm, LANES), 1)
    lg = jnp.where(lane < n_experts, logits, -jnp.inf)
    m1 = jnp.max(lg, axis=-1, keepdims=True)
    i1 = jnp.min(jnp.where(lg == m1, lane, LANES), axis=-1, keepdims=True)
    in_grp = ((lane // EXPERTS_PER_GROUP) == (i1 // EXPERTS_PER_GROUP)) & (lane != i1)
    lg2 = jnp.where(in_grp, lg, -jnp.inf)
    m2 = jnp.max(lg2, axis=-1, keepdims=True)
    i2 = jnp.min(jnp.where(lg2 == m2, lane, LANES), axis=-1, keepdims=True)
    e2 = jnp.exp(m2 - m1)
    g1 = 1.0 / (1.0 + e2)
    g2 = e2 / (1.0 + e2)
    oh1 = lane == i1
    oh2 = lane == i2
    ohs = jnp.where(oh1 | oh2, 1.0, 0.0)
    rr = lax.broadcasted_iota(jnp.int32, (tm, tm), 0)
    cc = lax.broadcasted_iota(jnp.int32, (tm, tm), 1)
    tri = jnp.where(cc < rr, 1.0, 0.0).astype(BF16)
    before = jnp.dot(tri, ohs.astype(BF16), preferred_element_type=F32) + cnt_ref[0:1, :]
    rank1 = jnp.sum(jnp.where(oh1, before, 0.0), axis=-1, keepdims=True)
    rank2 = jnp.sum(jnp.where(oh2, before, 0.0), axis=-1, keepdims=True)
    cnt_ref[...] = cnt_ref[...] + jnp.sum(ohs, axis=0, keepdims=True)
    rec = jnp.zeros((tm, LANES), F32)
    for ln, v in ((R_E1, i1.astype(F32)), (R_E2, i2.astype(F32)), (R_RANK1, rank1), (R_RANK2, rank2),
                  (R_G1, g1), (R_G2, g2)):
        rec = jnp.where(lane == ln, v, rec)
    route_ref[...] = rec


def _pack_bf16_pairs(x):
    half = x.shape[1] // 2
    lo = lax.bitcast_convert_type(x[:, :half].astype(BF16).astype(F32), U32)
    hi = lax.bitcast_convert_type(x[:, half:].astype(BF16).astype(F32), U32)
    return lax.shift_right_logical(lo, jnp.uint32(16)) | (hi & jnp.uint32(0xFFFF0000))


def _unpack_bf16_pairs(xp):
    lo = lax.bitcast_convert_type(lax.shift_left(xp, jnp.uint32(16)), F32)
    hi = lax.bitcast_convert_type(xp & jnp.uint32(0xFFFF0000), F32)
    return jnp.concatenate([lo, hi], axis=1).astype(BF16)


def _tail_kernel(*refs, n_mix, alpha, n_experts):
    mix_refs = refs[:n_mix]
    (hres_ref, pg_ref, pb_ref, w_ref, g_ref, b_ref, rw_ref, rb_ref, h_ref, hp_ref, route_ref, cnt_ref) = refs[n_mix:]

    @pl.when(pl.program_id(0) == 0)
    def _():
        cnt_ref[...] = jnp.zeros_like(cnt_ref)

    if n_mix == 1:
        mixin = mix_refs[0][...]
        hres = hres_ref[...]
    else:
        mixin = _combine_groups(mix_refs[:N_DIL_GROUPS], mix_refs[N_DIL_GROUPS:])
        hres = _layer_norm_rows(hres_ref[...], pg_ref[...], pb_ref[...])
    mix = jnp.dot(mixin, w_ref[...], preferred_element_type=F32)
    h1 = _layer_norm_rows(alpha * hres + mix, g_ref[...], b_ref[...])
    h_ref[...] = h1
    hp_ref[...] = _pack_bf16_pairs(h1)
    _route(h1, rw_ref, rb_ref, cnt_ref, route_ref, n_experts)


def _mixer_tail(mix_inputs, hres, pre_g, pre_b, w_bf16, g, b, rw_cat, rb_pad, alpha, n_experts):
    T, D = hres.shape
    tm = TOK_TM
    assert T % tm == 0
    kw = w_bf16.shape[0]
    n_mix = len(mix_inputs)
    mix_specs = [pl.BlockSpec((tm, a.shape[1]), lambda i: (i, 0)) if a.ndim == 2
                 else pl.BlockSpec((a.shape[0], tm, a.shape[2]), lambda i: (0, i, 0)) for a in mix_inputs]
    kern = functools.partial(_tail_kernel, n_mix=n_mix, alpha=alpha, n_experts=n_experts)
    return pl.pallas_call(
        kern,
        out_shape=(jax.ShapeDtypeStruct((T, D), F32), jax.ShapeDtypeStruct((T, D // 2), U32),
                   jax.ShapeDtypeStruct((T, LANES), F32), jax.ShapeDtypeStruct((SUBLANES, LANES), F32)),
        grid=(T // tm,),
        in_specs=mix_specs + [
            pl.BlockSpec((tm, D), lambda i: (i, 0)),
            pl.BlockSpec((1, D), lambda i: (0, 0)),
            pl.BlockSpec((1, D), lambda i: (0, 0)),
            pl.BlockSpec((kw, D), lambda i: (0, 0)),
            pl.BlockSpec((1, D), lambda i: (0, 0)),
            pl.BlockSpec((1, D), lambda i: (0, 0)),
            pl.BlockSpec((D, LANES), lambda i: (0, 0)),
            pl.BlockSpec((1, LANES), lambda i: (0, 0)),
        ],
        out_specs=(
            pl.BlockSpec((tm, D), lambda i: (i, 0)),
            pl.BlockSpec((tm, D // 2), lambda i: (i, 0)),
            pl.BlockSpec((tm, LANES), lambda i: (i, 0)),
            pl.BlockSpec((SUBLANES, LANES), lambda i: (0, 0)),
        ),
        compiler_params=_cparams(("arbitrary",)),
        name="mixer_tail_router",
    )(*mix_inputs, hres, pre_g, pre_b, w_bf16, g, b, rw_cat, rb_pad)


def _plan_kernel(route_ref, cnt_ref, dest_ref, plan_ref, *, tm_e, n_experts):
    tm = route_ref.shape[0]
    lane8 = lax.broadcasted_iota(jnp.int32, (SUBLANES, LANES), 1)
    cnt = cnt_ref[...]
    padded = jnp.floor((cnt + (tm_e - 1)) * (1.0 / tm_e)) * tm_e
    pad_end = padded
    k = 1
    while k < LANES:
        pad_end = pad_end + jnp.where(lane8 >= k, pltpu.roll(pad_end, k, 1), 0.0)
        k *= 2
    pad_start = pad_end - padded

    rec = route_ref[...]
    lane = lax.broadcasted_iota(jnp.int32, (tm, LANES), 1)
    lanef = lane.astype(F32)
    ps = pad_start[0:1, :]
    d1 = jnp.sum(jnp.where(lanef == rec[:, R_E1:R_E1 + 1], ps, 0.0), axis=-1, keepdims=True) + rec[:, R_RANK1:R_RANK1 + 1]
    d2 = jnp.sum(jnp.where(lanef == rec[:, R_E2:R_E2 + 1], ps, 0.0), axis=-1, keepdims=True) + rec[:, R_RANK2:R_RANK2 + 1]
    both = jnp.where(lane == 0, d1, jnp.where(lane == 1, d2, 0.0))
    bt = both.T
    dest_ref[0] = jnp.concatenate([bt[0:1, :], bt[1:2, :]], axis=1).astype(jnp.int32)

    @pl.when(pl.program_id(0) == 0)
    def _():
        sub8 = lax.broadcasted_iota(jnp.int32, (SUBLANES, LANES), 0)
        tile_row = ((sub8 * LANES + lane8) * tm_e).astype(F32)
        te = jnp.zeros((SUBLANES, LANES), F32)
        for e in range(n_experts):
            te = te + jnp.where(pad_end[:, e:e + 1] <= tile_row, 1.0, 0.0)
        te = jnp.minimum(te, float(n_experts - 1))
        nv = pad_end[:, n_experts - 1:n_experts] * (1.0 / tm_e)
        plan_ref[...] = jnp.where(sub8 == PLAN_NV_ROW, nv, te).astype(jnp.int32)


def _dispatch_plan(route, counts, n_experts, n_tiles):
    T = route.shape[0]
    tm = TOK_TM
    assert n_tiles <= PLAN_NV_ROW * LANES
    dest, plan = pl.pallas_call(
        functools.partial(_plan_kernel, tm_e=EXP_TM, n_experts=n_experts),
        out_shape=(jax.ShapeDtypeStruct((T // tm, 1, 2 * tm), jnp.int32),
                   jax.ShapeDtypeStruct((SUBLANES, LANES), jnp.int32)),
        grid=(T // tm,),
        in_specs=[pl.BlockSpec((tm, LANES), lambda i: (i, 0)), pl.BlockSpec((SUBLANES, LANES), lambda i: (0, 0))],
        out_specs=(pl.BlockSpec((1, 1, 2 * tm), lambda i: (i, 0, 0)),
                   pl.BlockSpec((SUBLANES, LANES), lambda i: (0, 0))),
        compiler_params=_cparams(("arbitrary",)),
        name="dispatch_plan",
    )(route, counts)
    tile_expert = plan[:PLAN_NV_ROW].reshape(-1)[:n_tiles]
    n_valid = plan[PLAN_NV_ROW, 0:1]
    return dest, tile_expert, n_valid


def _row_copy(src_ref, src_row, dst_ref, dst_row, sem):
    return pltpu.make_async_copy(src_ref.at[pl.ds(src_row, 1)], dst_ref.at[pl.ds(dst_row, 1)], sem)


def _dispatch_kernel(dest_ref, h_ref, xs_in_ref, xs_ref, sem):
    del xs_in_ref
    tm = h_ref.shape[0]

    def issue(r, carry):
        _row_copy(h_ref, r, xs_ref, dest_ref[0, 0, r], sem).start()
        _row_copy(h_ref, r, xs_ref, dest_ref[0, 0, tm + r], sem).start()
        return carry

    lax.fori_loop(0, tm, issue, 0)

    def drain(r, carry):
        _row_copy(h_ref, 0, xs_ref, 0, sem).wait()
        _row_copy(h_ref, 0, xs_ref, 0, sem).wait()
        return carry

    lax.fori_loop(0, tm, drain, 0)


def _dispatch(hp, dest_tiles, n_rows):
    T, W = hp.shape
    tm = TOK_TM
    xs0 = jnp.zeros((n_rows, W), hp.dtype)
    return pl.pallas_call(
        _dispatch_kernel,
        out_shape=jax.ShapeDtypeStruct((n_rows, W), hp.dtype),
        grid=(T // tm,),
        in_specs=[
            pl.BlockSpec((1, 1, 2 * tm), lambda i: (i, 0, 0), memory_space=pltpu.SMEM),
            pl.BlockSpec((tm, W), lambda i: (i, 0)),
            pl.BlockSpec(memory_space=pl.ANY),
        ],
        out_specs=pl.BlockSpec(memory_space=pl.ANY),
        scratch_shapes=[pltpu.SemaphoreType.DMA(())],
        input_output_aliases={2: 0},
        compiler_params=_cparams(("arbitrary",)),
        name="expert_dispatch",
    )(dest_tiles, hp, xs0)


def _gmm_kernel(te_ref, nv_ref, *refs, mode, packed):
    if mode == "swiglu":
        x_ref, w_ref, a_ref, o_ref, wbf_ref = refs
    else:
        x_ref, w_ref, o_ref, wbf_ref = refs
    j = pl.program_id(0)

    @pl.when(j < nv_ref[0])
    def _():
        prev = te_ref[jnp.maximum(j - 1, 0)]

        @pl.when((j == 0) | (te_ref[j] != prev))
        def _():
            wbf_ref[...] = w_ref[0, 0].astype(BF16)

        x = _unpack_bf16_pairs(x_ref[...]) if packed else x_ref[...]
        acc = jnp.dot(x, wbf_ref[...], preferred_element_type=F32)
        if mode == "swiglu":
            a = a_ref[...].astype(F32)
            acc = a * (1.0 / (1.0 + jnp.exp(-a))) * acc
        o_ref[...] = acc.astype(o_ref.dtype)

    @pl.when(j >= nv_ref[0])
    def _():
        o_ref[...] = jnp.zeros_like(o_ref)


def _gmm(x, w, layer, tile_expert, n_valid, out_dtype, a=None):
    P, kx = x.shape
    K, N = w.shape[-2:]
    packed = x.dtype == U32
    assert kx == (K // 2 if packed else K)
    tm = EXP_TM
    n_tiles = P // tm
    mode = "plain" if a is None else "swiglu"

    def row_map(j, te, nv):
        return (jnp.minimum(j, nv[0] - 1), 0)

    def w_map(j, te, nv):
        return (layer, te[jnp.minimum(j, nv[0] - 1)], 0, 0)

    in_specs = [pl.BlockSpec((tm, kx), row_map), pl.BlockSpec((1, 1, K, N), w_map)]
    args = [x, w]
    if a is not None:
        in_specs.append(pl.BlockSpec((tm, N), row_map))
        args.append(a)
    grid_spec = pltpu.PrefetchScalarGridSpec(
        num_scalar_prefetch=2,
        grid=(n_tiles,),
        in_specs=in_specs,
        out_specs=pl.BlockSpec((tm, N), lambda j, te, nv: (j, 0)),
        scratch_shapes=[pltpu.VMEM((K, N), BF16)],
    )
    return pl.pallas_call(
        functools.partial(_gmm_kernel, mode=mode, packed=packed),
        out_shape=jax.ShapeDtypeStruct((P, N), out_dtype),
        grid_spec=grid_spec,
        compiler_params=_cparams(("arbitrary",)),
        name=f"expert_matmul_{mode}",
    )(tile_expert, n_valid, *args)


def _combine_kernel(dest_ref, route_ref, hres_ref, y_ref, g_ref, b_ref, h_ref, hb_ref, buf_ref, sem, *, alpha):
    tm = hres_ref.shape[0]

    def issue(r, carry):
        _row_copy(y_ref, dest_ref[0, 0, r], buf_ref.at[0], r, sem).start()
        _row_copy(y_ref, dest_ref[0, 0, tm + r], buf_ref.at[1], r, sem).start()
        return carry

    lax.fori_loop(0, tm, issue, 0)

    def drain(r, carry):
        _row_copy(y_ref, 0, buf_ref.at[0], 0, sem).wait()
        _row_copy(y_ref, 0, buf_ref.at[1], 0, sem).wait()
        return carry

    lax.fori_loop(0, tm, drain, 0)
    rec = route_ref[...]
    f = rec[:, R_G1:R_G1 + 1] * buf_ref[0] + rec[:, R_G2:R_G2 + 1] * buf_ref[1]
    h2 = _layer_norm_rows(alpha * hres_ref[...] + f, g_ref[...], b_ref[...])
    h_ref[...] = h2
    hb_ref[...] = h2.astype(BF16)


def _combine(dest_tiles, route, hres, y, g, b, alpha):
    T, D = hres.shape
    tm = TOK_TM
    return pl.pallas_call(
        functools.partial(_combine_kernel, alpha=alpha),
        out_shape=(jax.ShapeDtypeStruct((T, D), F32), jax.ShapeDtypeStruct((T, D), BF16)),
        grid=(T // tm,),
        in_specs=[
            pl.BlockSpec((1, 1, 2 * tm), lambda i: (i, 0, 0), memory_space=pltpu.SMEM),
            pl.BlockSpec((tm, LANES), lambda i: (i, 0)),
            pl.BlockSpec((tm, D), lambda i: (i, 0)),
            pl.BlockSpec(memory_space=pl.ANY),
            pl.BlockSpec((1, D), lambda i: (0, 0)),
            pl.BlockSpec((1, D), lambda i: (0, 0)),
        ],
        out_specs=(
            pl.BlockSpec((tm, D), lambda i: (i, 0)),
            pl.BlockSpec((tm, D), lambda i: (i, 0)),
        ),
        scratch_shapes=[pltpu.VMEM((2, tm, D), F32), pltpu.SemaphoreType.DMA(())],
        compiler_params=_cparams(("arbitrary",)),
        name="expert_combine_ln",
    )(dest_tiles, route, hres, y, g, b)


def _grouped_moe(h, hp, route, counts, w_gate, w_up, w_down, layer, g, b, alpha):
    T, D = h.shape
    E = w_gate.shape[1]
    n_tiles = (T * 2) // EXP_TM + E
    dest_tiles, tile_expert, n_valid = _dispatch_plan(route, counts, E, n_tiles)
    xs = _dispatch(hp, dest_tiles, n_tiles * EXP_TM)
    a = _gmm(xs, w_gate, layer, tile_expert, n_valid, BF16)
    hid = _gmm(xs, w_up, layer, tile_expert, n_valid, BF16, a=a)
    y = _gmm(hid, w_down, layer, tile_expert, n_valid, F32)
    return _combine(dest_tiles, route, h, y, g, b, alpha)


def _fourier_a_kernel(x_ref, cs_ref, f_ref, yr_ref, yi_ref, *, gd):
    n1 = x_ref.shape[0]
    zr, zi = [], []
    for gi in range(FOURIER_GROUPS):
        pq = jnp.dot(x_ref[:, gi * gd:(gi + 1) * gd], cs_ref[...], preferred_element_type=F32)
        zr.append(pq[:, :gd])
        zi.append(pq[:, gd:])
    z = jnp.concatenate([jnp.concatenate(zr, axis=1), jnp.concatenate(zi, axis=1)], axis=0).astype(BF16)
    y = jnp.dot(f_ref[0], z, preferred_element_type=F32)
    yr_ref[...] = y[:n1].astype(BF16)
    yi_ref[...] = y[n1:].astype(BF16)


def _fourier_b_kernel(yr_ref, yi_ref, f_ref, o_ref, *, norm):
    y = jnp.concatenate([yr_ref[...], yi_ref[...]], axis=0)
    o_ref[...] = (jnp.dot(f_ref[...], y, preferred_element_type=F32) * norm).astype(BF16)


def _dft_tables(S, gd):
    n1, n2 = S // SEQ_N2, SEQ_N2
    jc = (jnp.arange(gd)[:, None] * jnp.arange(gd)[None, :]) % gd
    ang = jc.astype(F32) * (2.0 * math.pi / gd)
    cs = jnp.concatenate([jnp.cos(ang), -jnp.sin(ang)], axis=1).astype(BF16)
    k1 = jnp.arange(n1)[None, :, None]
    s1 = jnp.arange(n1)[None, None, :]
    s2 = jnp.arange(n2)[:, None, None]
    th = ((k1 * (n2 * s1 + s2)) % S).astype(F32) * (2.0 * math.pi / S)
    fr, fi = jnp.cos(th), -jnp.sin(th)
    f1 = jnp.concatenate([jnp.concatenate([fr, -fi], axis=2), jnp.concatenate([fi, fr], axis=2)], axis=1).astype(BF16)
    kk = (jnp.arange(n2)[:, None] * jnp.arange(n2)[None, :]) % n2
    th2 = kk.astype(F32) * (2.0 * math.pi / n2)
    f2 = jnp.concatenate([jnp.cos(th2), jnp.sin(th2)], axis=1).astype(BF16)
    return cs, f1, f2


def _fourier_mixed(hb, B, S):
    Tt, D = hb.shape
    gd = D // FOURIER_GROUPS
    n1, n2 = S // SEQ_N2, SEQ_N2
    cs, f1, f2 = _dft_tables(S, gd)
    xv = hb.reshape(B * n1, n2 * D)
    yr, yi = pl.pallas_call(
        functools.partial(_fourier_a_kernel, gd=gd),
        out_shape=(jax.ShapeDtypeStruct((B * n1, n2 * D), BF16),) * 2,
        grid=(B, n2),
        in_specs=[
            pl.BlockSpec((n1, D), lambda b, s: (b, s)),
            pl.BlockSpec((gd, 2 * gd), lambda b, s: (0, 0)),
            pl.BlockSpec((1, 2 * n1, 2 * n1), lambda b, s: (s, 0, 0)),
        ],
        out_specs=(pl.BlockSpec((n1, D), lambda b, s: (b, s)),) * 2,
        compiler_params=_cparams(("arbitrary", "arbitrary")),
        name="fourier_channel_step1",
    )(xv, cs, f1)
    yr = yr.reshape(Tt, D)
    yi = yi.reshape(Tt, D)
    mixed = pl.pallas_call(
        functools.partial(_fourier_b_kernel, norm=1.0 / math.sqrt(S * gd)),
        out_shape=jax.ShapeDtypeStruct((B * n2, n1 * D), BF16),
        grid=(B, n1),
        in_specs=[
            pl.BlockSpec((n2, D), lambda b, k: (b * n1 + k, 0)),
            pl.BlockSpec((n2, D), lambda b, k: (b * n1 + k, 0)),
            pl.BlockSpec((n2, 2 * n2), lambda b, k: (0, 0)),
        ],
        out_specs=pl.BlockSpec((n2, D), lambda b, k: (b, k)),
        compiler_params=_cparams(("arbitrary", "arbitrary")),
        name="fourier_step2",
    )(yr, yi, f2)
    return mixed.reshape(Tt, D)


def _head_dim_order():
    rest = list(range(ROT_DIM, HEAD_DIM))
    gap = ROT_PARTNER - ROT_HALF
    return list(range(ROT_HALF)) + rest[:gap] + list(range(ROT_HALF, ROT_DIM)) + rest[gap:]


def _qk_column_order():
    order = _head_dim_order()
    cols = []
    for g in range(N_DIL_GROUPS):
        for c in range(3):
            for h in range(HEADS_PER_GROUP):
                base = (g * 3 + c) * ATTN_WIDTH + h * HEAD_DIM
                cols.extend(base + (order[l] if c < 2 else l) for l in range(HEAD_DIM))
    return np.asarray(cols, np.int32)


def _rotary_tables(trunks, T):
    inv_freq = ROPE_THETA ** (-jnp.arange(0, ROT_DIM, 2, dtype=F32) / ROT_DIM)
    pos = jnp.concatenate([jnp.tile(jnp.arange(S, dtype=F32), B) for _, B, S in trunks])
    ang = pos[:, None] * inv_freq[None, :]
    cos, sin = jnp.cos(ang), jnp.sin(ang)
    gap = ROT_PARTNER - ROT_HALF
    cos_t = jnp.concatenate([cos, jnp.ones((T, gap), F32), cos, jnp.ones((T, gap), F32)], axis=1)
    sin_t = jnp.concatenate([-sin, jnp.zeros((T, gap), F32), sin, jnp.zeros((T, gap), F32)], axis=1)
    return cos_t, sin_t


def kernel(x_prompt, x_sample, ln_in_g, ln_in_b, attn_w_in, attn_w_out, fourier_w_out, router_w, router_b,
           expert_w_gate, expert_w_up, expert_w_down, ln_g, ln_b):
    Bp, Sp, D = x_prompt.shape
    Bs, Ss, _ = x_sample.shape
    depth = ln_g.shape[0]
    E = router_w.shape[1]
    alpha = (2 * depth) ** 0.25
    trunks = ((0, Bp, Sp), (Bp * Sp, Bs, Ss))
    T = Bp * Sp + Bs * Ss
    assert 2 * E == LANES and T % QKV_TM == 0
    assert depth == 2, "layer 0 (attention) is fused with the input LayerNorm, layer 1 is the Fourier layer"

    x = jnp.concatenate([x_prompt.reshape(Bp * Sp, D), x_sample.reshape(Bs * Ss, D)], axis=0)
    row = lambda v: v.reshape(1, -1).astype(F32)
    rw_hi = router_w.astype(BF16)
    rw_lo = (router_w - rw_hi.astype(F32)).astype(BF16)
    rw_cat = jnp.concatenate([rw_hi, rw_lo], axis=1)
    rb_pad = jnp.concatenate([router_b.astype(F32), jnp.zeros((LANES - E,), F32)]).reshape(1, LANES)
    cos_t, sin_t = _rotary_tables(trunks, T)
    gin, bin_ = row(ln_in_g), row(ln_in_b)

    w_in = jnp.take(attn_w_in[0], _qk_column_order(), axis=1).astype(BF16)
    qkv = _ln_qkv(x, gin, bin_, w_in, cos_t, sin_t)
    outs, lses = [], []
    for g, (_, dil) in enumerate(DILATION_PATTERNS):
        o, l = _attention_group(qkv, g, dil, trunks)
        outs.append(o)
        lses.append(l)
    h, hp, route, counts = _mixer_tail(outs + lses, x, gin, bin_, attn_w_out[0].astype(BF16),
                                       row(ln_g[0, 0]), row(ln_b[0, 0]), rw_cat, rb_pad, alpha, E)
    h, hb = _grouped_moe(h, hp, route, counts, expert_w_gate, expert_w_up, expert_w_down, 0,
                         row(ln_g[0, 1]), row(ln_b[0, 1]), alpha)

    parts = [_fourier_mixed(lax.slice_in_dim(hb, base, base + B * S, axis=0), B, S) for base, B, S in trunks]
    mixed = jnp.concatenate(parts, axis=0)
    h, hp, route, counts = _mixer_tail([mixed], h, gin, bin_, fourier_w_out[0].astype(BF16),
                                       row(ln_g[1, 0]), row(ln_b[1, 0]), rw_cat, rb_pad, alpha, E)
    h, hb = _grouped_moe(h, hp, route, counts, expert_w_gate, expert_w_up, expert_w_down, 1,
                         row(ln_g[1, 1]), row(ln_b[1, 1]), alpha)
    y_prompt = h[:Bp * Sp].reshape(Bp, Sp, D)
    y_sample = h[Bp * Sp:].reshape(Bs, Ss, D)
    return (y_prompt, y_sample)
```

```python
import functools
import math

import numpy as np
import jax
import jax.numpy as jnp
from jax import lax
from jax.experimental import pallas as pl
from jax.experimental.pallas import tpu as pltpu

F32 = jnp.float32
BF16 = jnp.bfloat16
U32 = jnp.uint32

DILATION_PATTERNS = ((128, 1), (512, 4), (2048, 16))
N_DIL_GROUPS = len(DILATION_PATTERNS)
HEAD_DIM = 128
HEADS_PER_GROUP = 8
ATTN_WIDTH = HEADS_PER_GROUP * HEAD_DIM
ROT_DIM = HEAD_DIM // 4
ROT_HALF = ROT_DIM // 2
ROT_PARTNER = HEAD_DIM // 2
ROPE_THETA = 500000.0
FOURIER_GROUPS = 8
EXPERTS_PER_GROUP = 8
LN_EPS = 1e-5
NEG_BIG = -1e30

LANES = 128
SUBLANES = 8
VMEM_LIMIT_BYTES = 56 * 1024 * 1024

QKV_TM = 1024
QKV_TN = ATTN_WIDTH
ATT_BLK = 64
ATT_SUB = 4
LN_TM = 512
ROW_TILE = SUBLANES
TOK_TM = 256
EXP_TM = 512
SEQ_N2 = 128

R_E1, R_E2, R_RANK1, R_RANK2, R_G1, R_G2 = 0, 1, 2, 3, 4, 5
PLAN_NV_ROW = SUBLANES - 1


def _cparams(sem, vmem=VMEM_LIMIT_BYTES):
    return pltpu.CompilerParams(dimension_semantics=sem, vmem_limit_bytes=vmem)


def _layer_norm_rows(x, g, b):
    mu = jnp.mean(x, axis=-1, keepdims=True)
    xc = x - mu
    var = jnp.mean(xc * xc, axis=-1, keepdims=True)
    return xc * lax.rsqrt(var + LN_EPS) * g + b


def _ln_in_kernel(xp_ref, xs_ref, g_ref, b_ref, h_ref, xn_ref, *, n_first):
    x = jnp.where(pl.program_id(0) < n_first, xp_ref[...], xs_ref[...])
    y = _layer_norm_rows(x, g_ref[...], b_ref[...])
    h_ref[...] = y
    xn_ref[...] = y.astype(BF16)


def _ln_in(xp, xs, g, b):
    Tp, D = xp.shape
    Ts = xs.shape[0]
    tm = LN_TM
    assert Tp % tm == 0 and Ts % tm == 0
    n_first = Tp // tm
    T = Tp + Ts
    return pl.pallas_call(
        functools.partial(_ln_in_kernel, n_first=n_first),
        out_shape=(jax.ShapeDtypeStruct((T, D), F32), jax.ShapeDtypeStruct((T, D), BF16)),
        grid=(T // tm,),
        in_specs=[
            pl.BlockSpec((tm, D), lambda i: (jnp.minimum(i, n_first - 1), 0)),
            pl.BlockSpec((tm, D), lambda i: (jnp.maximum(i - n_first, 0), 0)),
            pl.BlockSpec((1, D), lambda i: (0, 0)),
            pl.BlockSpec((1, D), lambda i: (0, 0)),
        ],
        out_specs=(pl.BlockSpec((tm, D), lambda i: (i, 0)), pl.BlockSpec((tm, D), lambda i: (i, 0))),
        compiler_params=_cparams(("arbitrary",)),
        name="ln_in",
    )(xp, xs, g, b)


def _qkv_kernel(x_ref, w_ref, c_ref, s_ref, o_ref, perm_ref, *, dil):
    acc = jnp.dot(x_ref[...], w_ref[...], preferred_element_type=F32)
    tm, tn = acc.shape
    chunk = tm // dil
    for cb in range(tn // LANES):
        ls = slice(cb * LANES, (cb + 1) * LANES)
        val = acc[:, ls]
        val = val * c_ref[0] + pltpu.roll(val, ROT_PARTNER, 1) * s_ref[0]
        if dil == 1:
            o_ref[:, ls] = val.astype(BF16)
        else:
            perm_ref[cb] = val
            for r in range(dil):
                o_ref[r * chunk:(r + 1) * chunk, ls] = perm_ref[cb, pl.ds(r, chunk, stride=dil), :].astype(BF16)


def _qkv_group(xn, w_bf16, g, dil, cos3, sin3):
    T, D = xn.shape
    tm, tn = QKV_TM, QKV_TN
    assert T % tm == 0
    return pl.pallas_call(
        functools.partial(_qkv_kernel, dil=dil),
        out_shape=jax.ShapeDtypeStruct((T, 3 * tn), BF16),
        grid=(T // tm, 3),
        in_specs=[
            pl.BlockSpec((tm, D), lambda i, c: (i, 0)),
            pl.BlockSpec((D, tn), lambda i, c: (0, g * 3 + c)),
            pl.BlockSpec((1, tm, LANES), lambda i, c: (c, i, 0)),
            pl.BlockSpec((1, tm, LANES), lambda i, c: (c, i, 0)),
        ],
        out_specs=pl.BlockSpec((tm, tn), lambda i, c: (i, c)),
        scratch_shapes=[pltpu.VMEM((tn // LANES, tm, LANES), F32)],
        compiler_params=_cparams(("arbitrary", "arbitrary")),
        name=f"qkv_proj_d{dil}",
    )(xn, w_bf16, cos3, sin3)


def _attn_kernel(qb_ref, kp_ref, kn_ref, ob_ref, rr_ref, var_ref,
                 q_ref, kc_ref, kp_blk, kn_blk, vc_ref, vp_blk, vn_blk, bias_ref, o_ref, l_ref,
                 *, dil, split_residues):
    u = pl.program_id(0)
    rr = rr_ref[u]
    blk = ATT_BLK
    sb = blk * ATT_SUB
    lane = lax.broadcasted_iota(jnp.int32, (sb, LANES), 1)
    bias = bias_ref[0]
    lse_tile = jnp.zeros((sb, LANES), F32)

    def store(ref, lead, val):
        if dil == 1:
            ref[lead + (slice(None), slice(None))] = val
        elif split_residues:
            for a in range(ATT_SUB):
                dst = pl.ds(rr * ATT_SUB + a, blk, stride=dil)
                ref[lead + (dst, slice(None))] = val[a * blk:(a + 1) * blk]
        else:
            ref[lead + (pl.ds(rr, sb, stride=dil), slice(None))] = val

    for h in range(HEADS_PER_GROUP):
        hs = slice(h * HEAD_DIM, (h + 1) * HEAD_DIM)
        kh = jnp.concatenate([kp_blk[:, hs], kc_ref[:, hs], kn_blk[:, hs]], axis=0)
        vh = jnp.concatenate([vp_blk[:, hs], vc_ref[:, hs], vn_blk[:, hs]], axis=0)
        s = lax.dot_general(q_ref[:, hs], kh, (((1,), (1,)), ((), ())), preferred_element_type=F32) + bias
        m = jnp.max(s, axis=-1, keepdims=True)
        p = jnp.exp(s - m)
        den = jnp.sum(p, axis=-1, keepdims=True)
        o = jnp.dot(p.astype(BF16), vh, preferred_element_type=F32) / den
        lse_tile = jnp.where(lane == h, m + jnp.log(den), lse_tile)
        store(o_ref, (h,), o)
    store(l_ref, (), lse_tile)


def _attn_bias(split_residues):
    blk, sb = ATT_BLK, ATT_BLK * ATT_SUB
    halo = sb if split_residues else blk
    row = np.arange(sb)[:, None]
    col = np.arange(halo + sb + halo)[None, :]
    if split_residues:
        tile, a_k, j_k = col // sb, (col % sb) // blk, col % blk
        ok = (a_k == row // blk) & (np.abs(blk * (tile - 1) + j_k - row % blk) <= blk)
        has_prev, has_next = tile != 0, tile != 2
    else:
        ok = np.abs(col - halo - row) <= blk
        has_prev, has_next = col >= halo, col < halo + sb
    variants = []
    for v in range(4):
        okv = ok & (has_prev | (v & 1 == 0)) & (has_next | (v & 2 == 0))
        variants.append(np.where(okv, 0.0, NEG_BIG))
    return np.stack(variants).astype(np.float32)


def _attn_units(trunks, dil, split_residues):
    sb = ATT_BLK * ATT_SUB
    chunk = QKV_TM // dil
    cols = [[] for _ in range(6)]
    for base, B, S in trunks:
        assert base % QKV_TM == 0 and S % QKV_TM == 0
        L = S // dil
        nb = L // ATT_BLK
        for b in range(B):
            t0 = (base + b * S) // QKV_TM
            if split_residues:
                assert chunk == ATT_BLK and dil % ATT_SUB == 0
                for n in range(nb):
                    for rq in range(dil // ATT_SUB):
                        per = QKV_TM // sb
                        cur = (t0 + n) * per + rq
                        prev = (t0 + max(n - 1, 0)) * per + rq
                        nxt = (t0 + min(n + 1, nb - 1)) * per + rq
                        variant = (n == 0) + 2 * (n == nb - 1)
                        vals = (cur, prev, nxt, (base + b * S) // QKV_TM + n, rq, variant)
                        for cl, v in zip(cols, vals):
                            cl.append(v)
            else:
                assert chunk % sb == 0 and L % sb == 0
                for m in range(L // sb):
                    for r in range(dil):
                        def row_of(n):
                            n = min(max(n, 0), nb - 1)
                            tt = t0 + (ATT_BLK * n * dil) // QKV_TM
                            return tt * QKV_TM + r * chunk + (ATT_BLK * n) % chunk
                        n0 = m * ATT_SUB
                        variant = (n0 == 0) + 2 * (n0 + ATT_SUB == nb)
                        vals = (row_of(n0) // sb, row_of(n0 - 1) // ATT_BLK, row_of(n0 + ATT_SUB) // ATT_BLK,
                                (base + b * S) // (sb * dil) + m, r, variant)
                        for cl, v in zip(cols, vals):
                            cl.append(v)
    return [np.asarray(cl, np.int32) for cl in cols]


def _attention_group(qkv, dil, trunks):
    T = qkv.shape[0]
    sb = ATT_BLK * ATT_SUB
    split_residues = (QKV_TM // dil) == ATT_BLK
    meta = _attn_units(trunks, dil, split_residues)
    n_units = len(meta[0])
    halo = sb if split_residues else ATT_BLK
    out_rows = QKV_TM if split_residues else sb * dil

    def spec(rows, which, c):
        return pl.BlockSpec((rows, ATTN_WIDTH), lambda u, *refs: (refs[which][u], c))

    bias = jnp.asarray(_attn_bias(split_residues))
    kern = functools.partial(_attn_kernel, dil=dil, split_residues=split_residues)
    grid_spec = pltpu.PrefetchScalarGridSpec(
        num_scalar_prefetch=6,
        grid=(n_units,),
        in_specs=[spec(sb, 0, 0), spec(sb, 0, 1), spec(halo, 1, 1), spec(halo, 2, 1),
                  spec(sb, 0, 2), spec(halo, 1, 2), spec(halo, 2, 2),
                  pl.BlockSpec((1,) + bias.shape[1:], lambda u, *refs: (refs[5][u], 0, 0))],
        out_specs=(
            pl.BlockSpec((HEADS_PER_GROUP, out_rows, HEAD_DIM), lambda u, *refs: (0, refs[3][u], 0)),
            pl.BlockSpec((out_rows, LANES), lambda u, *refs: (refs[3][u], 0)),
        ),
    )
    return pl.pallas_call(
        kern,
        out_shape=(jax.ShapeDtypeStruct((HEADS_PER_GROUP, T, HEAD_DIM), F32),
                   jax.ShapeDtypeStruct((T, LANES), F32)),
        grid_spec=grid_spec,
        compiler_params=_cparams(("arbitrary",)),
        name=f"band_attn_d{dil}",
    )(*[jnp.asarray(m) for m in meta], qkv, qkv, qkv, qkv, qkv, qkv, qkv, bias)


def _combine_groups(o_refs, l_refs):
    ls = [l[...] for l in l_refs]
    m = functools.reduce(jnp.maximum, ls)
    es = [jnp.exp(l - m) for l in ls]
    inv = 1.0 / functools.reduce(lambda a, b: a + b, es)
    ws = [e * inv for e in es]
    parts = []
    for h in range(HEADS_PER_GROUP):
        acc = None
        for w, o in zip(ws, o_refs):
            term = w[:, h:h + 1] * o[h]
            acc = term if acc is None else acc + term
        parts.append(acc)
    return jnp.concatenate(parts, axis=1).astype(BF16)


def _route(h1, rw_ref, rb_ref, cnt_ref, route_ref, n_experts):
    tm = h1.shape[0]
    a_hi = h1.astype(BF16)
    a_lo = (h1 - a_hi.astype(F32)).astype(BF16)
    r1 = jnp.dot(a_hi, rw_ref[...], preferred_element_type=F32)
    r2 = jnp.dot(a_lo, rw_ref[...], preferred_element_type=F32)
    tot = r1 + r2
    logits = tot + pltpu.roll(tot, LANES - n_experts, 1) + rb_ref[...]
    lane = lax.broadcasted_iota(jnp.int32, (tm, LANES), 1)
    lg = jnp.where(lane < n_experts, logits, -jnp.inf)
    m1 = jnp.max(lg, axis=-1, keepdims=True)
    i1 = jnp.min(jnp.where(lg == m1, lane, LANES), axis=-1, keepdims=True)
    in_grp = ((lane // EXPERTS_PER_GROUP) == (i1 // EXPERTS_PER_GROUP)) & (lane != i1)
    lg2 = jnp.where(in_grp, lg, -jnp.inf)
    m2 = jnp.max(lg2, axis=-1, keepdims=True)
    i2 = jnp.min(jnp.where(lg2 == m2, lane, LANES), axis=-1, keepdims=True)
    e2 = jnp.exp(m2 - m1)
    g1 = 1.0 / (1.0 + e2)
    g2 = e2 / (1.0 + e2)
    oh1 = lane == i1
    oh2 = lane == i2
    ohs = jnp.where(oh1 | oh2, 1.0, 0.0)
    rr = lax.broadcasted_iota(jnp.int32, (tm, tm), 0)
    cc = lax.broadcasted_iota(jnp.int32, (tm, tm), 1)
    tri = jnp.where(cc < rr, 1.0, 0.0).astype(BF16)
    before = jnp.dot(tri, ohs.astype(BF16), preferred_element_type=F32) + cnt_ref[0:1, :]
    rank1 = jnp.sum(jnp.where(oh1, before, 0.0), axis=-1, keepdims=True)
    rank2 = jnp.sum(jnp.where(oh2, before, 0.0), axis=-1, keepdims=True)
    cnt_ref[...] = cnt_ref[...] + jnp.sum(ohs, axis=0, keepdims=True)
    rec = jnp.zeros((tm, LANES), F32)
    for ln, v in ((R_E1, i1.astype(F32)), (R_E2, i2.astype(F32)), (R_RANK1, rank1), (R_RANK2, rank2),
                  (R_G1, g1), (R_G2, g2)):
        rec = jnp.where(lane == ln, v, rec)
    route_ref[...] = rec


def _store_packed_rows(ref, x):
    tm, d = x.shape
    half = d // 2
    assert half == ROW_TILE * LANES
    lo = lax.bitcast_convert_type(x[:, :half].astype(BF16).astype(F32), U32)
    hi = lax.bitcast_convert_type(x[:, half:].astype(BF16).astype(F32), U32)
    packed = lax.shift_right_logical(lo, jnp.uint32(16)) | (hi & jnp.uint32(0xFFFF0000))
    for s in range(ROW_TILE):
        ref[pl.ds(s, tm, stride=ROW_TILE), :] = packed[:, s * LANES:(s + 1) * LANES]


def _load_packed_rows(ref, n_rows):
    los, his = [], []
    for s in range(ROW_TILE):
        w = ref[pl.ds(s, n_rows, stride=ROW_TILE), :]
        los.append(lax.bitcast_convert_type(lax.shift_left(w, jnp.uint32(16)), F32))
        his.append(lax.bitcast_convert_type(w & jnp.uint32(0xFFFF0000), F32))
    return jnp.concatenate(los + his, axis=1)


def _tail_kernel(*refs, n_mix, alpha, n_experts):
    mix_refs = refs[:n_mix]
    (hres_ref, w_ref, g_ref, b_ref, rw_ref, rb_ref, h_ref, hp_ref, route_ref, cnt_ref) = refs[n_mix:]

    @pl.when(pl.program_id(0) == 0)
    def _():
        cnt_ref[...] = jnp.zeros_like(cnt_ref)

    if n_mix == 1:
        mixin = mix_refs[0][...]
    else:
        mixin = _combine_groups(mix_refs[:N_DIL_GROUPS], mix_refs[N_DIL_GROUPS:])
    mix = jnp.dot(mixin, w_ref[...], preferred_element_type=F32)
    h1 = _layer_norm_rows(alpha * hres_ref[...] + mix, g_ref[...], b_ref[...])
    h_ref[...] = h1
    _store_packed_rows(hp_ref, h1)
    _route(h1, rw_ref, rb_ref, cnt_ref, route_ref, n_experts)


def _mixer_tail(mix_inputs, hres, w_bf16, g, b, rw_cat, rb_pad, alpha, n_experts):
    T, D = hres.shape
    tm = TOK_TM
    assert T % tm == 0
    kw = w_bf16.shape[0]
    n_mix = len(mix_inputs)
    mix_specs = [pl.BlockSpec((tm, a.shape[1]), lambda i: (i, 0)) if a.ndim == 2
                 else pl.BlockSpec((a.shape[0], tm, a.shape[2]), lambda i: (0, i, 0)) for a in mix_inputs]
    kern = functools.partial(_tail_kernel, n_mix=n_mix, alpha=alpha, n_experts=n_experts)
    return pl.pallas_call(
        kern,
        out_shape=(jax.ShapeDtypeStruct((T, D), F32), jax.ShapeDtypeStruct((T * ROW_TILE, LANES), U32),
                   jax.ShapeDtypeStruct((T, LANES), F32), jax.ShapeDtypeStruct((SUBLANES, LANES), F32)),
        grid=(T // tm,),
        in_specs=mix_specs + [
            pl.BlockSpec((tm, D), lambda i: (i, 0)),
            pl.BlockSpec((kw, D), lambda i: (0, 0)),
            pl.BlockSpec((1, D), lambda i: (0, 0)),
            pl.BlockSpec((1, D), lambda i: (0, 0)),
            pl.BlockSpec((D, LANES), lambda i: (0, 0)),
            pl.BlockSpec((1, LANES), lambda i: (0, 0)),
        ],
        out_specs=(
            pl.BlockSpec((tm, D), lambda i: (i, 0)),
            pl.BlockSpec((tm * ROW_TILE, LANES), lambda i: (i, 0)),
            pl.BlockSpec((tm, LANES), lambda i: (i, 0)),
            pl.BlockSpec((SUBLANES, LANES), lambda i: (0, 0)),
        ),
        compiler_params=_cparams(("arbitrary",)),
        name="mixer_tail_router",
    )(*mix_inputs, hres, w_bf16, g, b, rw_cat, rb_pad)


def _plan_kernel(route_ref, cnt_ref, dest_ref, plan_ref, *, tm_e, n_experts):
    tm = route_ref.shape[0]
    lane8 = lax.broadcasted_iota(jnp.int32, (SUBLANES, LANES), 1)
    cnt = cnt_ref[...]
    padded = jnp.floor((cnt + (tm_e - 1)) * (1.0 / tm_e)) * tm_e
    pad_end = padded
    k = 1
    while k < LANES:
        pad_end = pad_end + jnp.where(lane8 >= k, pltpu.roll(pad_end, k, 1), 0.0)
        k *= 2
    pad_start = pad_end - padded

    rec = route_ref[...]
    lane = lax.broadcasted_iota(jnp.int32, (tm, LANES), 1)
    lanef = lane.astype(F32)
    ps = pad_start[0:1, :]
    d1 = jnp.sum(jnp.where(lanef == rec[:, R_E1:R_E1 + 1], ps, 0.0), axis=-1, keepdims=True) + rec[:, R_RANK1:R_RANK1 + 1]
    d2 = jnp.sum(jnp.where(lanef == rec[:, R_E2:R_E2 + 1], ps, 0.0), axis=-1, keepdims=True) + rec[:, R_RANK2:R_RANK2 + 1]
    both = jnp.where(lane == 0, d1, jnp.where(lane == 1, d2, 0.0))
    bt = both.T
    dest_ref[0] = jnp.concatenate([bt[0:1, :], bt[1:2, :]], axis=1).astype(jnp.int32)

    @pl.when(pl.program_id(0) == 0)
    def _():
        sub8 = lax.broadcasted_iota(jnp.int32, (SUBLANES, LANES), 0)
        tile_row = ((sub8 * LANES + lane8) * tm_e).astype(F32)
        te = jnp.zeros((SUBLANES, LANES), F32)
        for e in range(n_experts):
            te = te + jnp.where(pad_end[:, e:e + 1] <= tile_row, 1.0, 0.0)
        te = jnp.minimum(te, float(n_experts - 1))
        nv = pad_end[:, n_experts - 1:n_experts] * (1.0 / tm_e)
        plan_ref[...] = jnp.where(sub8 == PLAN_NV_ROW, nv, te).astype(jnp.int32)


def _dispatch_plan(route, counts, n_experts, n_tiles):
    T = route.shape[0]
    tm = TOK_TM
    assert n_tiles <= PLAN_NV_ROW * LANES
    dest, plan = pl.pallas_call(
        functools.partial(_plan_kernel, tm_e=EXP_TM, n_experts=n_experts),
        out_shape=(jax.ShapeDtypeStruct((T // tm, 1, 2 * tm), jnp.int32),
                   jax.ShapeDtypeStruct((SUBLANES, LANES), jnp.int32)),
        grid=(T // tm,),
        in_specs=[pl.BlockSpec((tm, LANES), lambda i: (i, 0)), pl.BlockSpec((SUBLANES, LANES), lambda i: (0, 0))],
        out_specs=(pl.BlockSpec((1, 1, 2 * tm), lambda i: (i, 0, 0)),
                   pl.BlockSpec((SUBLANES, LANES), lambda i: (0, 0))),
        compiler_params=_cparams(("arbitrary",)),
        name="dispatch_plan",
    )(route, counts)
    tile_expert = plan[:PLAN_NV_ROW].reshape(-1)[:n_tiles]
    n_valid = plan[PLAN_NV_ROW, 0:1]
    return dest, tile_expert, n_valid


def _row_copy(src_ref, src_row, dst_ref, dst_row, sem):
    src = src_ref.at[pl.ds(pl.multiple_of(src_row * ROW_TILE, ROW_TILE), ROW_TILE)]
    dst = dst_ref.at[pl.ds(pl.multiple_of(dst_row * ROW_TILE, ROW_TILE), ROW_TILE)]
    return pltpu.make_async_copy(src, dst, sem)


def _dispatch_kernel(dest_ref, h_ref, xs_in_ref, xs_ref, sem):
    del xs_in_ref
    tm = h_ref.shape[0] // ROW_TILE

    def issue(r, carry):
        _row_copy(h_ref, r, xs_ref, dest_ref[0, 0, r], sem).start()
        _row_copy(h_ref, r, xs_ref, dest_ref[0, 0, tm + r], sem).start()
        return carry

    lax.fori_loop(0, tm, issue, 0)

    def drain(r, carry):
        _row_copy(h_ref, 0, xs_ref, 0, sem).wait()
        _row_copy(h_ref, 0, xs_ref, 0, sem).wait()
        return carry

    lax.fori_loop(0, tm, drain, 0)


def _dispatch(hp, dest_tiles, n_rows):
    T = hp.shape[0] // ROW_TILE
    tm = TOK_TM
    xs0 = jnp.zeros((n_rows * ROW_TILE, LANES), hp.dtype)
    return pl.pallas_call(
        _dispatch_kernel,
        out_shape=jax.ShapeDtypeStruct(xs0.shape, hp.dtype),
        grid=(T // tm,),
        in_specs=[
            pl.BlockSpec((1, 1, 2 * tm), lambda i: (i, 0, 0), memory_space=pltpu.SMEM),
            pl.BlockSpec((tm * ROW_TILE, LANES), lambda i: (i, 0)),
            pl.BlockSpec(memory_space=pl.ANY),
        ],
        out_specs=pl.BlockSpec(memory_space=pl.ANY),
        scratch_shapes=[pltpu.SemaphoreType.DMA(())],
        input_output_aliases={2: 0},
        compiler_params=_cparams(("arbitrary",)),
        name="expert_dispatch",
    )(dest_tiles, hp, xs0)


def _gmm_kernel(te_ref, nv_ref, *refs, mode, tm, packed_in, packed_out):
    if mode == "swiglu":
        x_ref, w_ref, a_ref, o_ref, wbf_ref = refs
    else:
        x_ref, w_ref, o_ref, wbf_ref = refs
    j = pl.program_id(0)

    @pl.when(j < nv_ref[0])
    def _():
        prev = te_ref[jnp.maximum(j - 1, 0)]

        @pl.when((j == 0) | (te_ref[j] != prev))
        def _():
            wbf_ref[...] = w_ref[0, 0].astype(BF16)

        x = _load_packed_rows(x_ref, tm).astype(BF16) if packed_in else x_ref[...]
        acc = jnp.dot(x, wbf_ref[...], preferred_element_type=F32)
        if mode == "swiglu":
            a = a_ref[...].astype(F32)
            acc = a * (1.0 / (1.0 + jnp.exp(-a))) * acc
        if packed_out:
            _store_packed_rows(o_ref, acc)
        else:
            o_ref[...] = acc.astype(o_ref.dtype)

    @pl.when(j >= nv_ref[0])
    def _():
        o_ref[...] = jnp.zeros_like(o_ref)


def _gmm(x, w, layer, tile_expert, n_valid, n_tiles, a=None, packed_out=False):
    K, N = w.shape[-2:]
    packed_in = x.dtype == U32
    tm = EXP_TM
    mode = "plain" if a is None else "swiglu"

    def row_map(j, te, nv):
        return (jnp.minimum(j, nv[0] - 1), 0)

    def w_map(j, te, nv):
        return (layer, te[jnp.minimum(j, nv[0] - 1)], 0, 0)

    x_block = (tm * ROW_TILE, LANES) if packed_in else (tm, K)
    out_block = (tm * ROW_TILE, LANES) if packed_out else (tm, N)
    in_specs = [pl.BlockSpec(x_block, row_map), pl.BlockSpec((1, 1, K, N), w_map)]
    args = [x, w]
    if a is not None:
        in_specs.append(pl.BlockSpec((tm, N), row_map))
        args.append(a)
    grid_spec = pltpu.PrefetchScalarGridSpec(
        num_scalar_prefetch=2,
        grid=(n_tiles,),
        in_specs=in_specs,
        out_specs=pl.BlockSpec(out_block, lambda j, te, nv: (j, 0)),
        scratch_shapes=[pltpu.VMEM((K, N), BF16)],
    )
    out_shape = (n_tiles * out_block[0], out_block[1])
    return pl.pallas_call(
        functools.partial(_gmm_kernel, mode=mode, tm=tm, packed_in=packed_in, packed_out=packed_out),
        out_shape=jax.ShapeDtypeStruct(out_shape, U32 if packed_out else BF16),
        grid_spec=grid_spec,
        compiler_params=_cparams(("arbitrary",)),
        name=f"expert_matmul_{mode}",
    )(tile_expert, n_valid, *args)


def _combine_kernel(dest_ref, next_ref, route_ref, hres_ref, y_ref, g_ref, b_ref, o1_ref, o2_ref, buf_ref, sems,
                    *, alpha, n_first):
    tm = hres_ref.shape[0]
    i = pl.program_id(0)
    n = pl.num_programs(0)
    slot = lax.rem(i, 2)

    def gather(d_ref, s):
        def issue(r, carry):
            _row_copy(y_ref, d_ref[0, 0, r], buf_ref.at[s, 0], r, sems.at[s]).start()
            _row_copy(y_ref, d_ref[0, 0, tm + r], buf_ref.at[s, 1], r, sems.at[s]).start()
            return carry
        lax.fori_loop(0, tm, issue, 0)

    @pl.when(i == 0)
    def _():
        gather(dest_ref, 0)

    @pl.when(i + 1 < n)
    def _():
        gather(next_ref, 1 - slot)

    def drain(r, carry):
        _row_copy(y_ref, 0, buf_ref.at[slot, 0], 0, sems.at[slot]).wait()
        _row_copy(y_ref, 0, buf_ref.at[slot, 1], 0, sems.at[slot]).wait()
        return carry

    lax.fori_loop(0, tm, drain, 0)
    rec = route_ref[...]
    f = (rec[:, R_G1:R_G1 + 1] * _load_packed_rows(buf_ref.at[slot, 0], tm)
         + rec[:, R_G2:R_G2 + 1] * _load_packed_rows(buf_ref.at[slot, 1], tm))
    h2 = _layer_norm_rows(alpha * hres_ref[...] + f, g_ref[...], b_ref[...])
    if n_first is None:
        o1_ref[...] = h2
        o2_ref[...] = h2.astype(BF16)
    else:
        @pl.when(i < n_first)
        def _():
            o1_ref[...] = h2

        @pl.when(i >= n_first)
        def _():
            o2_ref[...] = h2


def _combine(dest_tiles, route, hres, y, g, b, alpha, split_rows=None):
    T, D = hres.shape
    tm = TOK_TM
    n_steps = T // tm
    if split_rows is None:
        n_first = None
        out_shape = (jax.ShapeDtypeStruct((T, D), F32), jax.ShapeDtypeStruct((T, D), BF16))
        out_specs = (pl.BlockSpec((tm, D), lambda i: (i, 0)), pl.BlockSpec((tm, D), lambda i: (i, 0)))
    else:
        assert split_rows % tm == 0
        n_first = split_rows // tm
        out_shape = (jax.ShapeDtypeStruct((split_rows, D), F32), jax.ShapeDtypeStruct((T - split_rows, D), F32))
        out_specs = (pl.BlockSpec((tm, D), lambda i: (jnp.minimum(i, n_first - 1), 0)),
                     pl.BlockSpec((tm, D), lambda i: (jnp.maximum(i - n_first, 0), 0)))
    return pl.pallas_call(
        functools.partial(_combine_kernel, alpha=alpha, n_first=n_first),
        out_shape=out_shape,
        grid=(n_steps,),
        in_specs=[
            pl.BlockSpec((1, 1, 2 * tm), lambda i: (i, 0, 0), memory_space=pltpu.SMEM),
            pl.BlockSpec((1, 1, 2 * tm), lambda i: (jnp.minimum(i + 1, n_steps - 1), 0, 0), memory_space=pltpu.SMEM),
            pl.BlockSpec((tm, LANES), lambda i: (i, 0)),
            pl.BlockSpec((tm, D), lambda i: (i, 0)),
            pl.BlockSpec(memory_space=pl.ANY),
            pl.BlockSpec((1, D), lambda i: (0, 0)),
            pl.BlockSpec((1, D), lambda i: (0, 0)),
        ],
        out_specs=out_specs,
        scratch_shapes=[pltpu.VMEM((2, 2, tm * ROW_TILE, LANES), U32), pltpu.SemaphoreType.DMA((2,))],
        compiler_params=_cparams(("arbitrary",)),
        name="expert_combine_ln",
    )(dest_tiles, dest_tiles, route, hres, y, g, b)


def _grouped_moe(h, hp, route, counts, w_gate, w_up, w_down, layer, g, b, alpha, split_rows=None):
    T, D = h.shape
    E = w_gate.shape[1]
    n_tiles = (T * 2) // EXP_TM + E
    dest_tiles, tile_expert, n_valid = _dispatch_plan(route, counts, E, n_tiles)
    xs = _dispatch(hp, dest_tiles, n_tiles * EXP_TM)
    a = _gmm(xs, w_gate, layer, tile_expert, n_valid, n_tiles)
    hid = _gmm(xs, w_up, layer, tile_expert, n_valid, n_tiles, a=a)
    y = _gmm(hid, w_down, layer, tile_expert, n_valid, n_tiles, packed_out=True)
    return _combine(dest_tiles, route, h, y, g, b, alpha, split_rows)


def _fourier_a_kernel(x_ref, cs_ref, f_ref, yr_ref, yi_ref, *, gd):
    n1 = x_ref.shape[0]
    zr, zi = [], []
    for gi in range(FOURIER_GROUPS):
        pq = jnp.dot(x_ref[:, gi * gd:(gi + 1) * gd], cs_ref[...], preferred_element_type=F32)
        zr.append(pq[:, :gd])
        zi.append(pq[:, gd:])
    z = jnp.concatenate([jnp.concatenate(zr, axis=1), jnp.concatenate(zi, axis=1)], axis=0).astype(BF16)
    y = jnp.dot(f_ref[0], z, preferred_element_type=F32)
    yr_ref[...] = y[:n1].astype(BF16)
    yi_ref[...] = y[n1:].astype(BF16)


def _fourier_b_kernel(yr_ref, yi_ref, f_ref, o_ref, *, norm):
    y = jnp.concatenate([yr_ref[...], yi_ref[...]], axis=0)
    o_ref[...] = (jnp.dot(f_ref[...], y, preferred_element_type=F32) * norm).astype(BF16)


def _dft_tables(S, gd):
    n1, n2 = S // SEQ_N2, SEQ_N2
    jc = (jnp.arange(gd)[:, None] * jnp.arange(gd)[None, :]) % gd
    ang = jc.astype(F32) * (2.0 * math.pi / gd)
    cs = jnp.concatenate([jnp.cos(ang), -jnp.sin(ang)], axis=1).astype(BF16)
    k1 = jnp.arange(n1)[None, :, None]
    s1 = jnp.arange(n1)[None, None, :]
    s2 = jnp.arange(n2)[:, None, None]
    th = ((k1 * (n2 * s1 + s2)) % S).astype(F32) * (2.0 * math.pi / S)
    fr, fi = jnp.cos(th), -jnp.sin(th)
    f1 = jnp.concatenate([jnp.concatenate([fr, -fi], axis=2), jnp.concatenate([fi, fr], axis=2)], axis=1).astype(BF16)
    kk = (jnp.arange(n2)[:, None] * jnp.arange(n2)[None, :]) % n2
    th2 = kk.astype(F32) * (2.0 * math.pi / n2)
    f2 = jnp.concatenate([jnp.cos(th2), jnp.sin(th2)], axis=1).astype(BF16)
    return cs, f1, f2


def _fourier_mixed(hb, B, S):
    Tt, D = hb.shape
    gd = D // FOURIER_GROUPS
    n1, n2 = S // SEQ_N2, SEQ_N2
    cs, f1, f2 = _dft_tables(S, gd)
    xv = hb.reshape(B * n1, n2 * D)
    yr, yi = pl.pallas_call(
        functools.partial(_fourier_a_kernel, gd=gd),
        out_shape=(jax.ShapeDtypeStruct((B * n1, n2 * D), BF16),) * 2,
        grid=(B, n2),
        in_specs=[
            pl.BlockSpec((n1, D), lambda b, s: (b, s)),
            pl.BlockSpec((gd, 2 * gd), lambda b, s: (0, 0)),
            pl.BlockSpec((1, 2 * n1, 2 * n1), lambda b, s: (s, 0, 0)),
        ],
        out_specs=(pl.BlockSpec((n1, D), lambda b, s: (b, s)),) * 2,
        compiler_params=_cparams(("arbitrary", "arbitrary")),
        name="fourier_channel_step1",
    )(xv, cs, f1)
    yr = yr.reshape(Tt, D)
    yi = yi.reshape(Tt, D)
    mixed = pl.pallas_call(
        functools.partial(_fourier_b_kernel, norm=1.0 / math.sqrt(S * gd)),
        out_shape=jax.ShapeDtypeStruct((B * n2, n1 * D), BF16),
        grid=(B, n1),
        in_specs=[
            pl.BlockSpec((n2, D), lambda b, k: (b * n1 + k, 0)),
            pl.BlockSpec((n2, D), lambda b, k: (b * n1 + k, 0)),
            pl.BlockSpec((n2, 2 * n2), lambda b, k: (0, 0)),
        ],
        out_specs=pl.BlockSpec((n2, D), lambda b, k: (b, k)),
        compiler_params=_cparams(("arbitrary", "arbitrary")),
        name="fourier_step2",
    )(yr, yi, f2)
    return mixed.reshape(Tt, D)


def _head_dim_order():
    rest = list(range(ROT_DIM, HEAD_DIM))
    gap = ROT_PARTNER - ROT_HALF
    return list(range(ROT_HALF)) + rest[:gap] + list(range(ROT_HALF, ROT_DIM)) + rest[gap:]


def _qk_column_order():
    order = _head_dim_order()
    cols = []
    for g in range(N_DIL_GROUPS):
        for c in range(3):
            for h in range(HEADS_PER_GROUP):
                base = (g * 3 + c) * ATTN_WIDTH + h * HEAD_DIM
                cols.extend(base + (order[l] if c < 2 else l) for l in range(HEAD_DIM))
    return np.asarray(cols, np.int32)


def _rotary_tables(trunks, T):
    inv_freq = ROPE_THETA ** (-jnp.arange(0, ROT_DIM, 2, dtype=F32) / ROT_DIM)
    pos = jnp.concatenate([jnp.tile(jnp.arange(S, dtype=F32), B) for _, B, S in trunks])
    ang = pos[:, None] * inv_freq[None, :]
    cos, sin = jnp.cos(ang), jnp.sin(ang)
    gap = ROT_PARTNER - ROT_HALF
    cos_t = jnp.concatenate([cos, jnp.ones((T, gap), F32), cos, jnp.ones((T, gap), F32)], axis=1)
    sin_t = jnp.concatenate([-sin, jnp.zeros((T, gap), F32), sin, jnp.zeros((T, gap), F32)], axis=1)
    q_scale = HEAD_DIM ** -0.5
    cos3 = jnp.stack([cos_t * q_scale, cos_t, jnp.ones_like(cos_t)])
    sin3 = jnp.stack([sin_t * q_scale, sin_t, jnp.zeros_like(sin_t)])
    return cos3, sin3


def kernel(x_prompt, x_sample, ln_in_g, ln_in_b, attn_w_in, attn_w_out, fourier_w_out, router_w, router_b,
           expert_w_gate, expert_w_up, expert_w_down, ln_g, ln_b):
    Bp, Sp, D = x_prompt.shape
    Bs, Ss, _ = x_sample.shape
    depth = ln_g.shape[0]
    E = router_w.shape[1]
    alpha = (2 * depth) ** 0.25
    trunks = ((0, Bp, Sp), (Bp * Sp, Bs, Ss))
    T = Bp * Sp + Bs * Ss
    assert 2 * E == LANES and T % QKV_TM == 0
    assert depth == 2, "layer 0 (attention) is fused with the input LayerNorm, layer 1 is the Fourier layer"

    row = lambda v: v.reshape(1, -1).astype(F32)
    rw_hi = router_w.astype(BF16)
    rw_lo = (router_w - rw_hi.astype(F32)).astype(BF16)
    rw_cat = jnp.concatenate([rw_hi, rw_lo], axis=1)
    rb_pad = jnp.concatenate([router_b.astype(F32), jnp.zeros((LANES - E,), F32)]).reshape(1, LANES)
    cos3, sin3 = _rotary_tables(trunks, T)

    h, xn = _ln_in(x_prompt.reshape(Bp * Sp, D), x_sample.reshape(Bs * Ss, D), row(ln_in_g), row(ln_in_b))
    w_in = jnp.take(attn_w_in[0].astype(BF16), _qk_column_order(), axis=1)
    outs, lses = [], []
    for g, (_, dil) in enumerate(DILATION_PATTERNS):
        o, l = _attention_group(_qkv_group(xn, w_in, g, dil, cos3, sin3), dil, trunks)
        outs.append(o)
        lses.append(l)
    h, hp, route, counts = _mixer_tail(outs + lses, h, attn_w_out[0].astype(BF16),
                                       row(ln_g[0, 0]), row(ln_b[0, 0]), rw_cat, rb_pad, alpha, E)
    h, hb = _grouped_moe(h, hp, route, counts, expert_w_gate, expert_w_up, expert_w_down, 0,
                         row(ln_g[0, 1]), row(ln_b[0, 1]), alpha)

    parts = [_fourier_mixed(lax.slice_in_dim(hb, base, base + B * S, axis=0), B, S) for base, B, S in trunks]
    mixed = jnp.concatenate(parts, axis=0)
    h, hp, route, counts = _mixer_tail([mixed], h, fourier_w_out[0].astype(BF16),
                                       row(ln_g[1, 0]), row(ln_b[1, 0]), rw_cat, rb_pad, alpha, E)
    y_p, y_s = _grouped_moe(h, hp, route, counts, expert_w_gate, expert_w_up, expert_w_down, 1,
                            row(ln_g[1, 1]), row(ln_b[1, 1]), alpha, split_rows=Bp * Sp)
    return (y_p.reshape(Bp, Sp, D), y_s.reshape(Bs, Ss, D))
```

```python
import functools
import math

import numpy as np
import jax
import jax.numpy as jnp
from jax import lax
from jax.experimental import pallas as pl
from jax.experimental.pallas import tpu as pltpu

F32 = jnp.float32
BF16 = jnp.bfloat16
U32 = jnp.uint32

DILATION_PATTERNS = ((128, 1), (512, 4), (2048, 16))
N_DIL_GROUPS = len(DILATION_PATTERNS)
HEAD_DIM = 128
HEADS_PER_GROUP = 8
ATTN_WIDTH = HEADS_PER_GROUP * HEAD_DIM
ROT_DIM = HEAD_DIM // 4
ROT_HALF = ROT_DIM // 2
ROT_PARTNER = HEAD_DIM // 2
ROPE_THETA = 500000.0
FOURIER_GROUPS = 8
EXPERTS_PER_GROUP = 8
LN_EPS = 1e-5
NEG_BIG = -1e30

LANES = 128
SUBLANES = 8
VMEM_LIMIT_BYTES = 56 * 1024 * 1024

QKV_TM = 1024
QKV_TN = ATTN_WIDTH
ATT_BLK = 64
ATT_SUB = 4
LN_TM = 512
ROW_TILE = SUBLANES
TOK_TM = 256
EXP_TM = 512
SEQ_SUB = SUBLANES
SEQ_N2 = 128

R_E1, R_E2, R_RANK1, R_RANK2, R_G1, R_G2 = 0, 1, 2, 3, 4, 5
PLAN_NV_ROW = SUBLANES - 1


def _cparams(sem, vmem=VMEM_LIMIT_BYTES):
    return pltpu.CompilerParams(dimension_semantics=sem, vmem_limit_bytes=vmem)


def _layer_norm_rows(x, g, b):
    mu = jnp.mean(x, axis=-1, keepdims=True)
    xc = x - mu
    var = jnp.mean(xc * xc, axis=-1, keepdims=True)
    return xc * lax.rsqrt(var + LN_EPS) * g + b


def _ln_in_kernel(xp_ref, xs_ref, g_ref, b_ref, h_ref, xn_ref, *, n_first):
    x = jnp.where(pl.program_id(0) < n_first, xp_ref[...], xs_ref[...])
    y = _layer_norm_rows(x, g_ref[...], b_ref[...])
    h_ref[...] = y
    xn_ref[...] = y.astype(BF16)


def _ln_in(xp, xs, g, b):
    Tp, D = xp.shape
    Ts = xs.shape[0]
    tm = LN_TM
    assert Tp % tm == 0 and Ts % tm == 0
    n_first = Tp // tm
    T = Tp + Ts
    return pl.pallas_call(
        functools.partial(_ln_in_kernel, n_first=n_first),
        out_shape=(jax.ShapeDtypeStruct((T, D), F32), jax.ShapeDtypeStruct((T, D), BF16)),
        grid=(T // tm,),
        in_specs=[
            pl.BlockSpec((tm, D), lambda i: (jnp.minimum(i, n_first - 1), 0)),
            pl.BlockSpec((tm, D), lambda i: (jnp.maximum(i - n_first, 0), 0)),
            pl.BlockSpec((1, D), lambda i: (0, 0)),
            pl.BlockSpec((1, D), lambda i: (0, 0)),
        ],
        out_specs=(pl.BlockSpec((tm, D), lambda i: (i, 0)), pl.BlockSpec((tm, D), lambda i: (i, 0))),
        compiler_params=_cparams(("arbitrary",)),
        name="ln_in",
    )(xp, xs, g, b)


def _qkv_kernel(x_ref, w_ref, c_ref, s_ref, o_ref, perm_ref, *, dil):
    acc = jnp.dot(x_ref[...], w_ref[...], preferred_element_type=F32)
    tm, tn = acc.shape
    chunk = tm // dil
    for cb in range(tn // LANES):
        ls = slice(cb * LANES, (cb + 1) * LANES)
        val = acc[:, ls]
        val = val * c_ref[0] + pltpu.roll(val, ROT_PARTNER, 1) * s_ref[0]
        if dil == 1:
            o_ref[:, ls] = val.astype(BF16)
        else:
            perm_ref[cb] = val
            for r in range(dil):
                o_ref[r * chunk:(r + 1) * chunk, ls] = perm_ref[cb, pl.ds(r, chunk, stride=dil), :].astype(BF16)


def _qkv_group(xn, w_bf16, g, dil, cos3, sin3):
    T, D = xn.shape
    tm, tn = QKV_TM, QKV_TN
    assert T % tm == 0
    return pl.pallas_call(
        functools.partial(_qkv_kernel, dil=dil),
        out_shape=jax.ShapeDtypeStruct((T, 3 * tn), BF16),
        grid=(T // tm, 3),
        in_specs=[
            pl.BlockSpec((tm, D), lambda i, c: (i, 0)),
            pl.BlockSpec((D, tn), lambda i, c: (0, g * 3 + c)),
            pl.BlockSpec((1, tm, LANES), lambda i, c: (c, i, 0)),
            pl.BlockSpec((1, tm, LANES), lambda i, c: (c, i, 0)),
        ],
        out_specs=pl.BlockSpec((tm, tn), lambda i, c: (i, c)),
        scratch_shapes=[pltpu.VMEM((tn // LANES, tm, LANES), F32)],
        compiler_params=_cparams(("arbitrary", "arbitrary")),
        name=f"qkv_proj_d{dil}",
    )(xn, w_bf16, cos3, sin3)


def _attn_kernel(qb_ref, kp_ref, kn_ref, ob_ref, rr_ref, var_ref,
                 q_ref, kc_ref, kp_blk, kn_blk, vc_ref, vp_blk, vn_blk, bias_ref, o_ref, l_ref,
                 *, dil, split_residues):
    u = pl.program_id(0)
    rr = rr_ref[u]
    blk = ATT_BLK
    sb = blk * ATT_SUB
    lane = lax.broadcasted_iota(jnp.int32, (sb, LANES), 1)
    bias = bias_ref[0]
    lse_tile = jnp.zeros((sb, LANES), F32)

    def store(ref, lead, val):
        if dil == 1:
            ref[lead + (slice(None), slice(None))] = val
        elif split_residues:
            for a in range(ATT_SUB):
                dst = pl.ds(rr * ATT_SUB + a, blk, stride=dil)
                ref[lead + (dst, slice(None))] = val[a * blk:(a + 1) * blk]
        else:
            ref[lead + (pl.ds(rr, sb, stride=dil), slice(None))] = val

    for h in range(HEADS_PER_GROUP):
        hs = slice(h * HEAD_DIM, (h + 1) * HEAD_DIM)
        kh = jnp.concatenate([kp_blk[:, hs], kc_ref[:, hs], kn_blk[:, hs]], axis=0)
        vh = jnp.concatenate([vp_blk[:, hs], vc_ref[:, hs], vn_blk[:, hs]], axis=0)
        s = lax.dot_general(q_ref[:, hs], kh, (((1,), (1,)), ((), ())), preferred_element_type=F32) + bias
        m = jnp.max(s, axis=-1, keepdims=True)
        p = jnp.exp(s - m)
        den = jnp.sum(p, axis=-1, keepdims=True)
        o = jnp.dot(p.astype(BF16), vh, preferred_element_type=F32) / den
        lse_tile = jnp.where(lane == h, m + jnp.log(den), lse_tile)
        store(o_ref, (h,), o)
    store(l_ref, (), lse_tile)


def _attn_bias(split_residues):
    blk, sb = ATT_BLK, ATT_BLK * ATT_SUB
    halo = sb if split_residues else blk
    row = np.arange(sb)[:, None]
    col = np.arange(halo + sb + halo)[None, :]
    if split_residues:
        tile, a_k, j_k = col // sb, (col % sb) // blk, col % blk
        ok = (a_k == row // blk) & (np.abs(blk * (tile - 1) + j_k - row % blk) <= blk)
        has_prev, has_next = tile != 0, tile != 2
    else:
        ok = np.abs(col - halo - row) <= blk
        has_prev, has_next = col >= halo, col < halo + sb
    variants = []
    for v in range(4):
        okv = ok & (has_prev | (v & 1 == 0)) & (has_next | (v & 2 == 0))
        variants.append(np.where(okv, 0.0, NEG_BIG))
    return np.stack(variants).astype(np.float32)


def _attn_units(trunks, dil, split_residues):
    sb = ATT_BLK * ATT_SUB
    chunk = QKV_TM // dil
    cols = [[] for _ in range(6)]
    for base, B, S in trunks:
        assert base % QKV_TM == 0 and S % QKV_TM == 0
        L = S // dil
        nb = L // ATT_BLK
        for b in range(B):
            t0 = (base + b * S) // QKV_TM
            if split_residues:
                assert chunk == ATT_BLK and dil % ATT_SUB == 0
                for n in range(nb):
                    for rq in range(dil // ATT_SUB):
                        per = QKV_TM // sb
                        cur = (t0 + n) * per + rq
                        prev = (t0 + max(n - 1, 0)) * per + rq
                        nxt = (t0 + min(n + 1, nb - 1)) * per + rq
                        variant = (n == 0) + 2 * (n == nb - 1)
                        vals = (cur, prev, nxt, (base + b * S) // QKV_TM + n, rq, variant)
                        for cl, v in zip(cols, vals):
                            cl.append(v)
            else:
                assert chunk % sb == 0 and L % sb == 0
                for m in range(L // sb):
                    for r in range(dil):
                        def row_of(n):
                            n = min(max(n, 0), nb - 1)
                            tt = t0 + (ATT_BLK * n * dil) // QKV_TM
                            return tt * QKV_TM + r * chunk + (ATT_BLK * n) % chunk
                        n0 = m * ATT_SUB
                        variant = (n0 == 0) + 2 * (n0 + ATT_SUB == nb)
                        vals = (row_of(n0) // sb, row_of(n0 - 1) // ATT_BLK, row_of(n0 + ATT_SUB) // ATT_BLK,
                                (base + b * S) // (sb * dil) + m, r, variant)
                        for cl, v in zip(cols, vals):
                            cl.append(v)
    return [np.asarray(cl, np.int32) for cl in cols]


def _attention_group(qkv, dil, trunks):
    T = qkv.shape[0]
    sb = ATT_BLK * ATT_SUB
    split_residues = (QKV_TM // dil) == ATT_BLK
    meta = _attn_units(trunks, dil, split_residues)
    n_units = len(meta[0])
    halo = sb if split_residues else ATT_BLK
    out_rows = QKV_TM if split_residues else sb * dil

    def spec(rows, which, c):
        return pl.BlockSpec((rows, ATTN_WIDTH), lambda u, *refs: (refs[which][u], c))

    bias = jnp.asarray(_attn_bias(split_residues))
    kern = functools.partial(_attn_kernel, dil=dil, split_residues=split_residues)
    grid_spec = pltpu.PrefetchScalarGridSpec(
        num_scalar_prefetch=6,
        grid=(n_units,),
        in_specs=[spec(sb, 0, 0), spec(sb, 0, 1), spec(halo, 1, 1), spec(halo, 2, 1),
                  spec(sb, 0, 2), spec(halo, 1, 2), spec(halo, 2, 2),
                  pl.BlockSpec((1,) + bias.shape[1:], lambda u, *refs: (refs[5][u], 0, 0))],
        out_specs=(
            pl.BlockSpec((HEADS_PER_GROUP, out_rows, HEAD_DIM), lambda u, *refs: (0, refs[3][u], 0)),
            pl.BlockSpec((out_rows, LANES), lambda u, *refs: (refs[3][u], 0)),
        ),
    )
    return pl.pallas_call(
        kern,
        out_shape=(jax.ShapeDtypeStruct((HEADS_PER_GROUP, T, HEAD_DIM), F32),
                   jax.ShapeDtypeStruct((T, LANES), F32)),
        grid_spec=grid_spec,
        compiler_params=_cparams(("arbitrary",)),
        name=f"band_attn_d{dil}",
    )(*[jnp.asarray(m) for m in meta], qkv, qkv, qkv, qkv, qkv, qkv, qkv, bias)


def _combine_groups(o_refs, l_refs):
    ls = [l[...] for l in l_refs]
    m = functools.reduce(jnp.maximum, ls)
    es = [jnp.exp(l - m) for l in ls]
    inv = 1.0 / functools.reduce(lambda a, b: a + b, es)
    ws = [e * inv for e in es]
    parts = []
    for h in range(HEADS_PER_GROUP):
        acc = None
        for w, o in zip(ws, o_refs):
            term = w[:, h:h + 1] * o[h]
            acc = term if acc is None else acc + term
        parts.append(acc)
    return jnp.concatenate(parts, axis=1).astype(BF16)


def _route(h1, rw_ref, rb_ref, cnt_ref, route_ref, n_experts):
    tm = h1.shape[0]
    a_hi = h1.astype(BF16)
    a_lo = (h1 - a_hi.astype(F32)).astype(BF16)
    r1 = jnp.dot(a_hi, rw_ref[...], preferred_element_type=F32)
    r2 = jnp.dot(a_lo, rw_ref[...], preferred_element_type=F32)
    tot = r1 + r2
    logits = tot + pltpu.roll(tot, LANES - n_experts, 1) + rb_ref[...]
    lane = lax.broadcasted_iota(jnp.int32, (tm, LANES), 1)
    lg = jnp.where(lane < n_experts, logits, -jnp.inf)
    m1 = jnp.max(lg, axis=-1, keepdims=True)
    i1 = jnp.min(jnp.where(lg == m1, lane, LANES), axis=-1, keepdims=True)
    in_grp = ((lane // EXPERTS_PER_GROUP) == (i1 // EXPERTS_PER_GROUP)) & (lane != i1)
    lg2 = jnp.where(in_grp, lg, -jnp.inf)
    m2 = jnp.max(lg2, axis=-1, keepdims=True)
    i2 = jnp.min(jnp.where(lg2 == m2, lane, LANES), axis=-1, keepdims=True)
    e2 = jnp.exp(m2 - m1)
    g1 = 1.0 / (1.0 + e2)
    g2 = e2 / (1.0 + e2)
    oh1 = lane == i1
    oh2 = lane == i2
    ohs = jnp.where(oh1 | oh2, 1.0, 0.0)
    rr = lax.broadcasted_iota(jnp.int32, (tm, tm), 0)
    cc = lax.broadcasted_iota(jnp.int32, (tm, tm), 1)
    tri = jnp.where(cc < rr, 1.0, 0.0).astype(BF16)
    before = jnp.dot(tri, ohs.astype(BF16), preferred_element_type=F32) + cnt_ref[0:1, :]
    rank1 = jnp.sum(jnp.where(oh1, before, 0.0), axis=-1, keepdims=True)
    rank2 = jnp.sum(jnp.where(oh2, before, 0.0), axis=-1, keepdims=True)
    cnt_ref[...] = cnt_ref[...] + jnp.sum(ohs, axis=0, keepdims=True)
    rec = jnp.zeros((tm, LANES), F32)
    for ln, v in ((R_E1, i1.astype(F32)), (R_E2, i2.astype(F32)), (R_RANK1, rank1), (R_RANK2, rank2),
                  (R_G1, g1), (R_G2, g2)):
        rec = jnp.where(lane == ln, v, rec)
    route_ref[...] = rec


def _store_packed_rows(ref, x):
    tm, d = x.shape
    half = d // 2
    assert half == ROW_TILE * LANES
    lo = lax.bitcast_convert_type(x[:, :half].astype(BF16).astype(F32), U32)
    hi = lax.bitcast_convert_type(x[:, half:].astype(BF16).astype(F32), U32)
    packed = lax.shift_right_logical(lo, jnp.uint32(16)) | (hi & jnp.uint32(0xFFFF0000))
    for s in range(ROW_TILE):
        ref[pl.ds(s, tm, stride=ROW_TILE), :] = packed[:, s * LANES:(s + 1) * LANES]


def _load_packed_rows(ref, n_rows):
    los, his = [], []
    for s in range(ROW_TILE):
        w = ref[pl.ds(s, n_rows, stride=ROW_TILE), :]
        los.append(lax.bitcast_convert_type(lax.shift_left(w, jnp.uint32(16)), F32))
        his.append(lax.bitcast_convert_type(w & jnp.uint32(0xFFFF0000), F32))
    return jnp.concatenate(los + his, axis=1)


def _tail_kernel(*refs, n_mix, n_first, alpha, n_experts):
    mix_refs = refs[:n_mix]
    (hres_ref, w_ref, g_ref, b_ref, rw_ref, rb_ref, h_ref, hp_ref, route_ref, cnt_ref) = refs[n_mix:]

    @pl.when(pl.program_id(0) == 0)
    def _():
        cnt_ref[...] = jnp.zeros_like(cnt_ref)

    if n_mix == 2:
        mixin = jnp.where(pl.program_id(0) < n_first, mix_refs[0][...], mix_refs[1][...]).astype(BF16)
    else:
        mixin = _combine_groups(mix_refs[:N_DIL_GROUPS], mix_refs[N_DIL_GROUPS:])
    mix = jnp.dot(mixin, w_ref[...], preferred_element_type=F32)
    h1 = _layer_norm_rows(alpha * hres_ref[...] + mix, g_ref[...], b_ref[...])
    h_ref[...] = h1
    _store_packed_rows(hp_ref, h1)
    _route(h1, rw_ref, rb_ref, cnt_ref, route_ref, n_experts)


def _mixer_tail(mix_inputs, hres, w_bf16, g, b, rw_cat, rb_pad, alpha, n_experts):
    T, D = hres.shape
    tm = TOK_TM
    assert T % tm == 0
    kw = w_bf16.shape[0]
    n_mix = len(mix_inputs)
    n_first = None
    if n_mix == 2:
        assert mix_inputs[0].shape[0] % tm == 0
        n_first = mix_inputs[0].shape[0] // tm
        mix_specs = [pl.BlockSpec((tm, D), lambda i: (jnp.minimum(i, n_first - 1), 0)),
                     pl.BlockSpec((tm, D), lambda i: (jnp.maximum(i - n_first, 0), 0))]
    else:
        mix_specs = [pl.BlockSpec((tm, a.shape[1]), lambda i: (i, 0)) if a.ndim == 2
                     else pl.BlockSpec((a.shape[0], tm, a.shape[2]), lambda i: (0, i, 0)) for a in mix_inputs]
    kern = functools.partial(_tail_kernel, n_mix=n_mix, n_first=n_first, alpha=alpha, n_experts=n_experts)
    return pl.pallas_call(
        kern,
        out_shape=(jax.ShapeDtypeStruct((T, D), F32), jax.ShapeDtypeStruct((T * ROW_TILE, LANES), U32),
                   jax.ShapeDtypeStruct((T, LANES), F32), jax.ShapeDtypeStruct((SUBLANES, LANES), F32)),
        grid=(T // tm,),
        in_specs=mix_specs + [
            pl.BlockSpec((tm, D), lambda i: (i, 0)),
            pl.BlockSpec((kw, D), lambda i: (0, 0)),
            pl.BlockSpec((1, D), lambda i: (0, 0)),
            pl.BlockSpec((1, D), lambda i: (0, 0)),
            pl.BlockSpec((D, LANES), lambda i: (0, 0)),
            pl.BlockSpec((1, LANES), lambda i: (0, 0)),
        ],
        out_specs=(
            pl.BlockSpec((tm, D), lambda i: (i, 0)),
            pl.BlockSpec((tm * ROW_TILE, LANES), lambda i: (i, 0)),
            pl.BlockSpec((tm, LANES), lambda i: (i, 0)),
            pl.BlockSpec((SUBLANES, LANES), lambda i: (0, 0)),
        ),
        compiler_params=_cparams(("arbitrary",)),
        name="mixer_tail_router",
    )(*mix_inputs, hres, w_bf16, g, b, rw_cat, rb_pad)


def _plan_kernel(route_ref, cnt_ref, dest_ref, plan_ref, *, tm_e, n_experts):
    tm = route_ref.shape[0]
    lane8 = lax.broadcasted_iota(jnp.int32, (SUBLANES, LANES), 1)
    cnt = cnt_ref[...]
    padded = jnp.floor((cnt + (tm_e - 1)) * (1.0 / tm_e)) * tm_e
    pad_end = padded
    k = 1
    while k < LANES:
        pad_end = pad_end + jnp.where(lane8 >= k, pltpu.roll(pad_end, k, 1), 0.0)
        k *= 2
    pad_start = pad_end - padded

    rec = route_ref[...]
    lane = lax.broadcasted_iota(jnp.int32, (tm, LANES), 1)
    lanef = lane.astype(F32)
    ps = pad_start[0:1, :]
    d1 = jnp.sum(jnp.where(lanef == rec[:, R_E1:R_E1 + 1], ps, 0.0), axis=-1, keepdims=True) + rec[:, R_RANK1:R_RANK1 + 1]
    d2 = jnp.sum(jnp.where(lanef == rec[:, R_E2:R_E2 + 1], ps, 0.0), axis=-1, keepdims=True) + rec[:, R_RANK2:R_RANK2 + 1]
    both = jnp.where(lane == 0, d1, jnp.where(lane == 1, d2, 0.0))
    bt = both.T
    dest_ref[0] = jnp.concatenate([bt[0:1, :], bt[1:2, :]], axis=1).astype(jnp.int32)

    @pl.when(pl.program_id(0) == 0)
    def _():
        sub8 = lax.broadcasted_iota(jnp.int32, (SUBLANES, LANES), 0)
        tile_row = ((sub8 * LANES + lane8) * tm_e).astype(F32)
        te = jnp.zeros((SUBLANES, LANES), F32)
        for e in range(n_experts):
            te = te + jnp.where(pad_end[:, e:e + 1] <= tile_row, 1.0, 0.0)
        te = jnp.minimum(te, float(n_experts - 1))
        nv = pad_end[:, n_experts - 1:n_experts] * (1.0 / tm_e)
        plan_ref[...] = jnp.where(sub8 == PLAN_NV_ROW, nv, te).astype(jnp.int32)


def _dispatch_plan(route, counts, n_experts, n_tiles):
    T = route.shape[0]
    tm = TOK_TM
    assert n_tiles <= PLAN_NV_ROW * LANES
    dest, plan = pl.pallas_call(
        functools.partial(_plan_kernel, tm_e=EXP_TM, n_experts=n_experts),
        out_shape=(jax.ShapeDtypeStruct((T // tm, 1, 2 * tm), jnp.int32),
                   jax.ShapeDtypeStruct((SUBLANES, LANES), jnp.int32)),
        grid=(T // tm,),
        in_specs=[pl.BlockSpec((tm, LANES), lambda i: (i, 0)), pl.BlockSpec((SUBLANES, LANES), lambda i: (0, 0))],
        out_specs=(pl.BlockSpec((1, 1, 2 * tm), lambda i: (i, 0, 0)),
                   pl.BlockSpec((SUBLANES, LANES), lambda i: (0, 0))),
        compiler_params=_cparams(("arbitrary",)),
        name="dispatch_plan",
    )(route, counts)
    tile_expert = plan[:PLAN_NV_ROW].reshape(-1)[:n_tiles]
    n_valid = plan[PLAN_NV_ROW, 0:1]
    return dest, tile_expert, n_valid


def _row_copy(src_ref, src_row, dst_ref, dst_row, sem):
    src = src_ref.at[pl.ds(pl.multiple_of(src_row * ROW_TILE, ROW_TILE), ROW_TILE)]
    dst = dst_ref.at[pl.ds(pl.multiple_of(dst_row * ROW_TILE, ROW_TILE), ROW_TILE)]
    return pltpu.make_async_copy(src, dst, sem)


def _dispatch_kernel(dest_ref, h_ref, xs_in_ref, xs_ref, sem):
    del xs_in_ref
    tm = h_ref.shape[0] // ROW_TILE

    def issue(r, carry):
        _row_copy(h_ref, r, xs_ref, dest_ref[0, 0, r], sem).start()
        _row_copy(h_ref, r, xs_ref, dest_ref[0, 0, tm + r], sem).start()
        return carry

    lax.fori_loop(0, tm, issue, 0)

    def drain(r, carry):
        _row_copy(h_ref, 0, xs_ref, 0, sem).wait()
        _row_copy(h_ref, 0, xs_ref, 0, sem).wait()
        return carry

    lax.fori_loop(0, tm, drain, 0)


def _dispatch(hp, dest_tiles, n_rows):
    T = hp.shape[0] // ROW_TILE
    tm = TOK_TM
    xs0 = jnp.zeros((n_rows * ROW_TILE, LANES), hp.dtype)
    return pl.pallas_call(
        _dispatch_kernel,
        out_shape=jax.ShapeDtypeStruct(xs0.shape, hp.dtype),
        grid=(T // tm,),
        in_specs=[
            pl.BlockSpec((1, 1, 2 * tm), lambda i: (i, 0, 0), memory_space=pltpu.SMEM),
            pl.BlockSpec((tm * ROW_TILE, LANES), lambda i: (i, 0)),
            pl.BlockSpec(memory_space=pl.ANY),
        ],
        out_specs=pl.BlockSpec(memory_space=pl.ANY),
        scratch_shapes=[pltpu.SemaphoreType.DMA(())],
        input_output_aliases={2: 0},
        compiler_params=_cparams(("arbitrary",)),
        name="expert_dispatch",
    )(dest_tiles, hp, xs0)


def _gmm_kernel(te_ref, nv_ref, *refs, mode, tm, packed_in, packed_out):
    if mode == "swiglu":
        x_ref, w_ref, a_ref, o_ref, wbf_ref = refs
    else:
        x_ref, w_ref, o_ref, wbf_ref = refs
    j = pl.program_id(0)

    @pl.when(j < nv_ref[0])
    def _():
        prev = te_ref[jnp.maximum(j - 1, 0)]

        @pl.when((j == 0) | (te_ref[j] != prev))
        def _():
            wbf_ref[...] = w_ref[0, 0].astype(BF16)

        x = _load_packed_rows(x_ref, tm).astype(BF16) if packed_in else x_ref[...]
        acc = jnp.dot(x, wbf_ref[...], preferred_element_type=F32)
        if mode == "swiglu":
            a = a_ref[...].astype(F32)
            acc = a * (1.0 / (1.0 + jnp.exp(-a))) * acc
        if packed_out:
            _store_packed_rows(o_ref, acc)
        else:
            o_ref[...] = acc.astype(o_ref.dtype)

    @pl.when(j >= nv_ref[0])
    def _():
        o_ref[...] = jnp.zeros_like(o_ref)


def _gmm(x, w, layer, tile_expert, n_valid, n_tiles, a=None, packed_out=False):
    K, N = w.shape[-2:]
    packed_in = x.dtype == U32
    tm = EXP_TM
    mode = "plain" if a is None else "swiglu"

    def row_map(j, te, nv):
        return (jnp.minimum(j, nv[0] - 1), 0)

    def w_map(j, te, nv):
        return (layer, te[jnp.minimum(j, nv[0] - 1)], 0, 0)

    x_block = (tm * ROW_TILE, LANES) if packed_in else (tm, K)
    out_block = (tm * ROW_TILE, LANES) if packed_out else (tm, N)
    in_specs = [pl.BlockSpec(x_block, row_map), pl.BlockSpec((1, 1, K, N), w_map)]
    args = [x, w]
    if a is not None:
        in_specs.append(pl.BlockSpec((tm, N), row_map))
        args.append(a)
    grid_spec = pltpu.PrefetchScalarGridSpec(
        num_scalar_prefetch=2,
        grid=(n_tiles,),
        in_specs=in_specs,
        out_specs=pl.BlockSpec(out_block, lambda j, te, nv: (j, 0)),
        scratch_shapes=[pltpu.VMEM((K, N), BF16)],
    )
    out_shape = (n_tiles * out_block[0], out_block[1])
    return pl.pallas_call(
        functools.partial(_gmm_kernel, mode=mode, tm=tm, packed_in=packed_in, packed_out=packed_out),
        out_shape=jax.ShapeDtypeStruct(out_shape, U32 if packed_out else BF16),
        grid_spec=grid_spec,
        compiler_params=_cparams(("arbitrary",)),
        name=f"expert_matmul_{mode}",
    )(tile_expert, n_valid, *args)


def _combine_kernel(dest_ref, next_ref, route_ref, hres_ref, y_ref, g_ref, b_ref, *refs, alpha, n_first):
    *out_refs, buf_ref, sems = refs
    o1_ref, o2_ref = out_refs[0], out_refs[-1]
    tm = hres_ref.shape[0]
    i = pl.program_id(0)
    n = pl.num_programs(0)

    def gather(d_ref, s):
        def issue(r, carry):
            _row_copy(y_ref, d_ref[0, 0, r], buf_ref.at[s, 0], r, sems.at[s]).start()
            _row_copy(y_ref, d_ref[0, 0, tm + r], buf_ref.at[s, 1], r, sems.at[s]).start()
            return carry
        lax.fori_loop(0, tm, issue, 0)

    def finish(s):
        def drain(r, carry):
            _row_copy(y_ref, 0, buf_ref.at[s, 0], 0, sems.at[s]).wait()
            _row_copy(y_ref, 0, buf_ref.at[s, 1], 0, sems.at[s]).wait()
            return carry

        lax.fori_loop(0, tm, drain, 0)
        rec = route_ref[...]
        f = (rec[:, R_G1:R_G1 + 1] * _load_packed_rows(buf_ref.at[s, 0], tm)
             + rec[:, R_G2:R_G2 + 1] * _load_packed_rows(buf_ref.at[s, 1], tm))
        h2 = _layer_norm_rows(alpha * hres_ref[...] + f, g_ref[...], b_ref[...])
        if n_first is None:
            o1_ref[...] = h2
        else:
            @pl.when(i < n_first)
            def _():
                o1_ref[...] = h2

            @pl.when(i >= n_first)
            def _():
                o2_ref[...] = h2

    @pl.when(i == 0)
    def _():
        gather(dest_ref, 0)

    for s in range(2):
        @pl.when(lax.rem(i, 2) == s)
        def _(s=s):
            @pl.when(i + 1 < n)
            def _():
                gather(next_ref, 1 - s)

            finish(s)


def _combine(dest_tiles, route, hres, y, g, b, alpha, split_rows=None):
    T, D = hres.shape
    tm = TOK_TM
    n_steps = T // tm
    if split_rows is None:
        n_first = None
        out_shape = (jax.ShapeDtypeStruct((T, D), F32),)
        out_specs = (pl.BlockSpec((tm, D), lambda i: (i, 0)),)
    else:
        assert split_rows % tm == 0
        n_first = split_rows // tm
        out_shape = (jax.ShapeDtypeStruct((split_rows, D), F32), jax.ShapeDtypeStruct((T - split_rows, D), F32))
        out_specs = (pl.BlockSpec((tm, D), lambda i: (jnp.minimum(i, n_first - 1), 0)),
                     pl.BlockSpec((tm, D), lambda i: (jnp.maximum(i - n_first, 0), 0)))
    return pl.pallas_call(
        functools.partial(_combine_kernel, alpha=alpha, n_first=n_first),
        out_shape=out_shape,
        grid=(n_steps,),
        in_specs=[
            pl.BlockSpec((1, 1, 2 * tm), lambda i: (i, 0, 0), memory_space=pltpu.SMEM),
            pl.BlockSpec((1, 1, 2 * tm), lambda i: (jnp.minimum(i + 1, n_steps - 1), 0, 0), memory_space=pltpu.SMEM),
            pl.BlockSpec((tm, LANES), lambda i: (i, 0)),
            pl.BlockSpec((tm, D), lambda i: (i, 0)),
            pl.BlockSpec(memory_space=pl.ANY),
            pl.BlockSpec((1, D), lambda i: (0, 0)),
            pl.BlockSpec((1, D), lambda i: (0, 0)),
        ],
        out_specs=out_specs,
        scratch_shapes=[pltpu.VMEM((2, 2, tm * ROW_TILE, LANES), U32), pltpu.SemaphoreType.DMA((2,))],
        compiler_params=_cparams(("arbitrary",)),
        name="expert_combine_ln",
    )(dest_tiles, dest_tiles, route, hres, y, g, b)


def _grouped_moe(h, hp, route, counts, w_gate, w_up, w_down, layer, g, b, alpha, split_rows=None):
    T, D = h.shape
    E = w_gate.shape[1]
    n_tiles = (T * 2) // EXP_TM + E
    dest_tiles, tile_expert, n_valid = _dispatch_plan(route, counts, E, n_tiles)
    xs = _dispatch(hp, dest_tiles, n_tiles * EXP_TM)
    a = _gmm(xs, w_gate, layer, tile_expert, n_valid, n_tiles)
    hid = _gmm(xs, w_up, layer, tile_expert, n_valid, n_tiles, a=a)
    y = _gmm(hid, w_down, layer, tile_expert, n_valid, n_tiles, packed_out=True)
    return _combine(dest_tiles, route, h, y, g, b, alpha, split_rows)


def _fourier_a_kernel(x_ref, cs_ref, f_ref, y_ref, *, gd):
    n1 = x_ref.shape[0]
    for jj in range(SEQ_SUB):
        x = x_ref[:, jj, :].astype(BF16)
        zr, zi = [], []
        for gi in range(FOURIER_GROUPS):
            pq = jnp.dot(x[:, gi * gd:(gi + 1) * gd], cs_ref[...], preferred_element_type=F32)
            zr.append(pq[:, :gd])
            zi.append(pq[:, gd:])
        z = jnp.concatenate([jnp.concatenate(zr, axis=1), jnp.concatenate(zi, axis=1)], axis=0).astype(BF16)
        y = jnp.dot(f_ref[jj], z, preferred_element_type=F32)
        re = lax.bitcast_convert_type(y[:n1].astype(BF16).astype(F32), U32)
        im = lax.bitcast_convert_type(y[n1:].astype(BF16).astype(F32), U32)
        y_ref[:, jj, :] = lax.shift_right_logical(re, jnp.uint32(16)) | (im & jnp.uint32(0xFFFF0000))


def _fourier_b_kernel(y_ref, f_ref, o_ref, *, norm):
    n2 = o_ref.shape[0]
    for jj in range(SEQ_SUB):
        w = y_ref[jj * n2:(jj + 1) * n2, :]
        yr = lax.bitcast_convert_type(lax.shift_left(w, jnp.uint32(16)), F32)
        yi = lax.bitcast_convert_type(w & jnp.uint32(0xFFFF0000), F32)
        y = jnp.concatenate([yr, yi], axis=0).astype(BF16)
        o_ref[:, jj, :] = jnp.dot(f_ref[...], y, preferred_element_type=F32) * norm


def _dft_tables(S, gd):
    n1, n2 = S // SEQ_N2, SEQ_N2
    jc = (jnp.arange(gd)[:, None] * jnp.arange(gd)[None, :]) % gd
    ang = jc.astype(F32) * (2.0 * math.pi / gd)
    cs = jnp.concatenate([jnp.cos(ang), -jnp.sin(ang)], axis=1).astype(BF16)
    k1 = jnp.arange(n1)[None, :, None]
    s1 = jnp.arange(n1)[None, None, :]
    s2 = jnp.arange(n2)[:, None, None]
    th = ((k1 * (n2 * s1 + s2)) % S).astype(F32) * (2.0 * math.pi / S)
    fr, fi = jnp.cos(th), -jnp.sin(th)
    f1 = jnp.concatenate([jnp.concatenate([fr, -fi], axis=2), jnp.concatenate([fi, fr], axis=2)], axis=1).astype(BF16)
    kk = (jnp.arange(n2)[:, None] * jnp.arange(n2)[None, :]) % n2
    th2 = kk.astype(F32) * (2.0 * math.pi / n2)
    f2 = jnp.concatenate([jnp.cos(th2), jnp.sin(th2)], axis=1).astype(BF16)
    return cs, f1, f2


def _fourier_mixed(h, base, B, S):
    T, D = h.shape
    gd = D // FOURIER_GROUPS
    n1, n2 = S // SEQ_N2, SEQ_N2
    assert n1 % SEQ_SUB == 0 and n2 % SEQ_SUB == 0 and base % S == 0
    cs, f1, f2 = _dft_tables(S, gd)
    off = base // S
    y = pl.pallas_call(
        functools.partial(_fourier_a_kernel, gd=gd),
        out_shape=jax.ShapeDtypeStruct((B * n1, n2, D), U32),
        grid=(B, n2 // SEQ_SUB),
        in_specs=[
            pl.BlockSpec((n1, SEQ_SUB, D), lambda b, s: (off + b, s, 0)),
            pl.BlockSpec((gd, 2 * gd), lambda b, s: (0, 0)),
            pl.BlockSpec((SEQ_SUB, 2 * n1, 2 * n1), lambda b, s: (s, 0, 0)),
        ],
        out_specs=pl.BlockSpec((n1, SEQ_SUB, D), lambda b, s: (b, s, 0)),
        compiler_params=_cparams(("arbitrary", "arbitrary")),
        name="fourier_channel_step1",
    )(h.reshape(T // n2, n2, D), cs, f1)
    mixed = pl.pallas_call(
        functools.partial(_fourier_b_kernel, norm=1.0 / math.sqrt(S * gd)),
        out_shape=jax.ShapeDtypeStruct((B * n2, n1, D), F32),
        grid=(B, n1 // SEQ_SUB),
        in_specs=[
            pl.BlockSpec((SEQ_SUB * n2, D), lambda b, k: (b * (n1 // SEQ_SUB) + k, 0)),
            pl.BlockSpec((n2, 2 * n2), lambda b, k: (0, 0)),
        ],
        out_specs=pl.BlockSpec((n2, SEQ_SUB, D), lambda b, k: (b, k, 0)),
        compiler_params=_cparams(("arbitrary", "arbitrary")),
        name="fourier_step2",
    )(y.reshape(B * S, D), f2)
    return mixed.reshape(B * S, D)


def _reorder_qk_columns(w):
    D = w.shape[0]
    nblk = HEAD_DIM // ROT_HALF
    gap = ROT_PARTNER // ROT_HALF
    order = [0] + list(range(2, gap + 1)) + [1] + list(range(gap + 1, nblk))
    w6 = w.reshape(D, N_DIL_GROUPS, 3, HEADS_PER_GROUP, nblk, ROT_HALF)
    qk = w6[:, :, :2]
    qk = jnp.concatenate([qk[:, :, :, :, i:i + 1] for i in order], axis=4)
    return jnp.concatenate([qk, w6[:, :, 2:]], axis=2).reshape(w.shape)


def _rotary_tables(trunks, T):
    inv_freq = ROPE_THETA ** (-jnp.arange(0, ROT_DIM, 2, dtype=F32) / ROT_DIM)
    pos = jnp.concatenate([jnp.tile(jnp.arange(S, dtype=F32), B) for _, B, S in trunks])
    ang = pos[:, None] * inv_freq[None, :]
    cos, sin = jnp.cos(ang), jnp.sin(ang)
    gap = ROT_PARTNER - ROT_HALF
    cos_t = jnp.concatenate([cos, jnp.ones((T, gap), F32), cos, jnp.ones((T, gap), F32)], axis=1)
    sin_t = jnp.concatenate([-sin, jnp.zeros((T, gap), F32), sin, jnp.zeros((T, gap), F32)], axis=1)
    q_scale = HEAD_DIM ** -0.5
    cos3 = jnp.stack([cos_t * q_scale, cos_t, jnp.ones_like(cos_t)])
    sin3 = jnp.stack([sin_t * q_scale, sin_t, jnp.zeros_like(sin_t)])
    return cos3, sin3


def kernel(x_prompt, x_sample, ln_in_g, ln_in_b, attn_w_in, attn_w_out, fourier_w_out, router_w, router_b,
           expert_w_gate, expert_w_up, expert_w_down, ln_g, ln_b):
    Bp, Sp, D = x_prompt.shape
    Bs, Ss, _ = x_sample.shape
    depth = ln_g.shape[0]
    E = router_w.shape[1]
    alpha = (2 * depth) ** 0.25
    trunks = ((0, Bp, Sp), (Bp * Sp, Bs, Ss))
    T = Bp * Sp + Bs * Ss
    assert 2 * E == LANES and T % QKV_TM == 0
    assert depth == 2, "layer 0 (attention) is fused with the input LayerNorm, layer 1 is the Fourier layer"

    row = lambda v: v.reshape(1, -1).astype(F32)
    rw_hi = router_w.astype(BF16)
    rw_lo = (router_w - rw_hi.astype(F32)).astype(BF16)
    rw_cat = jnp.concatenate([rw_hi, rw_lo], axis=1)
    rb_pad = jnp.concatenate([router_b.astype(F32), jnp.zeros((LANES - E,), F32)]).reshape(1, LANES)
    cos3, sin3 = _rotary_tables(trunks, T)

    h, xn = _ln_in(x_prompt.reshape(Bp * Sp, D), x_sample.reshape(Bs * Ss, D), row(ln_in_g), row(ln_in_b))
    w_in = _reorder_qk_columns(attn_w_in[0]).astype(BF16)
    outs, lses = [], []
    for g, (_, dil) in enumerate(DILATION_PATTERNS):
        o, l = _attention_group(_qkv_group(xn, w_in, g, dil, cos3, sin3), dil, trunks)
        outs.append(o)
        lses.append(l)
    h, hp, route, counts = _mixer_tail(outs + lses, h, attn_w_out[0].astype(BF16),
                                       row(ln_g[0, 0]), row(ln_b[0, 0]), rw_cat, rb_pad, alpha, E)
    (h,) = _grouped_moe(h, hp, route, counts, expert_w_gate, expert_w_up, expert_w_down, 0,
                        row(ln_g[0, 1]), row(ln_b[0, 1]), alpha)

    mixed = [_fourier_mixed(h, base, B, S) for base, B, S in trunks]
    h, hp, route, counts = _mixer_tail(mixed, h, fourier_w_out[0].astype(BF16),
                                       row(ln_g[1, 0]), row(ln_b[1, 0]), rw_cat, rb_pad, alpha, E)
    y_p, y_s = _grouped_moe(h, hp, route, counts, expert_w_gate, expert_w_up, expert_w_down, 1,
                            row(ln_g[1, 1]), row(ln_b[1, 1]), alpha, split_rows=Bp * Sp)
    return (y_p.reshape(Bp, Sp, D), y_s.reshape(Bs, Ss, D))
```

```python
import functools
import math

import numpy as np
import jax
import jax.numpy as jnp
from jax import lax
from jax.experimental import pallas as pl
from jax.experimental.pallas import tpu as pltpu

F32 = jnp.float32
BF16 = jnp.bfloat16
U32 = jnp.uint32

DILATION_PATTERNS = ((128, 1), (512, 4), (2048, 16))
N_DIL_GROUPS = len(DILATION_PATTERNS)
HEAD_DIM = 128
HEADS_PER_GROUP = 8
ATTN_WIDTH = HEADS_PER_GROUP * HEAD_DIM
ROT_DIM = HEAD_DIM // 4
ROT_HALF = ROT_DIM // 2
ROT_PARTNER = HEAD_DIM // 2
ROPE_THETA = 500000.0
FOURIER_GROUPS = 8
EXPERTS_PER_GROUP = 8
LN_EPS = 1e-5
NEG_BIG = -1e30

LANES = 128
SUBLANES = 8
VMEM_LIMIT_BYTES = 56 * 1024 * 1024

QKV_TM = 1024
QKV_TN = ATTN_WIDTH
ATT_BLK = 64
ATT_SUB = 4
LN_TM = 512
ROW_TILE = SUBLANES
TOK_TM = 256
EXP_TM = 512
SEQ_SUB = SUBLANES
SEQ_N2 = 128

R_E1, R_E2, R_RANK1, R_RANK2, R_G1, R_G2 = 0, 1, 2, 3, 4, 5
PLAN_NV_ROW = SUBLANES - 1


def _cparams(sem, vmem=VMEM_LIMIT_BYTES):
    return pltpu.CompilerParams(dimension_semantics=sem, vmem_limit_bytes=vmem)


def _layer_norm_rows(x, g, b):
    mu = jnp.mean(x, axis=-1, keepdims=True)
    xc = x - mu
    var = jnp.mean(xc * xc, axis=-1, keepdims=True)
    return xc * lax.rsqrt(var + LN_EPS) * g + b


def _ln_in_kernel(xp_ref, xs_ref, g_ref, b_ref, h_ref, xn_ref, *, n_first):
    x = jnp.where(pl.program_id(0) < n_first, xp_ref[...], xs_ref[...])
    y = _layer_norm_rows(x, g_ref[...], b_ref[...])
    h_ref[...] = y
    xn_ref[...] = y.astype(BF16)


def _ln_in(xp, xs, g, b):
    Tp, D = xp.shape
    Ts = xs.shape[0]
    tm = LN_TM
    assert Tp % tm == 0 and Ts % tm == 0
    n_first = Tp // tm
    T = Tp + Ts
    return pl.pallas_call(
        functools.partial(_ln_in_kernel, n_first=n_first),
        out_shape=(jax.ShapeDtypeStruct((T, D), F32), jax.ShapeDtypeStruct((T, D), BF16)),
        grid=(T // tm,),
        in_specs=[
            pl.BlockSpec((tm, D), lambda i: (jnp.minimum(i, n_first - 1), 0)),
            pl.BlockSpec((tm, D), lambda i: (jnp.maximum(i - n_first, 0), 0)),
            pl.BlockSpec((1, D), lambda i: (0, 0)),
            pl.BlockSpec((1, D), lambda i: (0, 0)),
        ],
        out_specs=(pl.BlockSpec((tm, D), lambda i: (i, 0)), pl.BlockSpec((tm, D), lambda i: (i, 0))),
        compiler_params=_cparams(("arbitrary",)),
        name="ln_in",
    )(xp, xs, g, b)


def _qkv_kernel(x_ref, w_ref, c_ref, s_ref, o_ref, perm_ref, *, dil):
    acc = jnp.dot(x_ref[...], w_ref[...], preferred_element_type=F32)
    tm, tn = acc.shape
    chunk = tm // dil
    for cb in range(tn // LANES):
        ls = slice(cb * LANES, (cb + 1) * LANES)
        val = acc[:, ls]
        val = val * c_ref[0] + pltpu.roll(val, ROT_PARTNER, 1) * s_ref[0]
        if dil == 1:
            o_ref[:, ls] = val.astype(BF16)
        else:
            perm_ref[cb] = val
            for r in range(dil):
                o_ref[r * chunk:(r + 1) * chunk, ls] = perm_ref[cb, pl.ds(r, chunk, stride=dil), :].astype(BF16)


def _qkv_group(xn, w_bf16, g, dil, cos3, sin3):
    T, D = xn.shape
    tm, tn = QKV_TM, QKV_TN
    assert T % tm == 0
    return pl.pallas_call(
        functools.partial(_qkv_kernel, dil=dil),
        out_shape=jax.ShapeDtypeStruct((T, 3 * tn), BF16),
        grid=(T // tm, 3),
        in_specs=[
            pl.BlockSpec((tm, D), lambda i, c: (i, 0)),
            pl.BlockSpec((D, tn), lambda i, c: (0, g * 3 + c)),
            pl.BlockSpec((1, tm, LANES), lambda i, c: (c, i, 0)),
            pl.BlockSpec((1, tm, LANES), lambda i, c: (c, i, 0)),
        ],
        out_specs=pl.BlockSpec((tm, tn), lambda i, c: (i, c)),
        scratch_shapes=[pltpu.VMEM((tn // LANES, tm, LANES), F32)],
        compiler_params=_cparams(("arbitrary", "arbitrary")),
        name=f"qkv_proj_d{dil}",
    )(xn, w_bf16, cos3, sin3)


def _attn_kernel(qb_ref, kp_ref, kn_ref, ob_ref, rr_ref, var_ref,
                 q_ref, kc_ref, kp_blk, kn_blk, vc_ref, vp_blk, vn_blk, bias_ref, o_ref, l_ref,
                 *, dil, split_residues):
    u = pl.program_id(0)
    rr = rr_ref[u]
    blk = ATT_BLK
    sb = blk * ATT_SUB
    lane = lax.broadcasted_iota(jnp.int32, (sb, LANES), 1)
    bias = bias_ref[0]
    lse_tile = jnp.zeros((sb, LANES), F32)

    def store(ref, lead, val):
        if dil == 1:
            ref[lead + (slice(None), slice(None))] = val
        elif split_residues:
            for a in range(ATT_SUB):
                dst = pl.ds(rr * ATT_SUB + a, blk, stride=dil)
                ref[lead + (dst, slice(None))] = val[a * blk:(a + 1) * blk]
        else:
            ref[lead + (pl.ds(rr, sb, stride=dil), slice(None))] = val

    for h in range(HEADS_PER_GROUP):
        hs = slice(h * HEAD_DIM, (h + 1) * HEAD_DIM)
        kh = jnp.concatenate([kp_blk[:, hs], kc_ref[:, hs], kn_blk[:, hs]], axis=0)
        vh = jnp.concatenate([vp_blk[:, hs], vc_ref[:, hs], vn_blk[:, hs]], axis=0)
        s = lax.dot_general(q_ref[:, hs], kh, (((1,), (1,)), ((), ())), preferred_element_type=F32) + bias
        m = jnp.max(s, axis=-1, keepdims=True)
        p = jnp.exp(s - m)
        den = jnp.sum(p, axis=-1, keepdims=True)
        o = jnp.dot(p.astype(BF16), vh, preferred_element_type=F32) / den
        lse_tile = jnp.where(lane == h, m + jnp.log(den), lse_tile)
        store(o_ref, (h,), o)
    store(l_ref, (), lse_tile)


def _attn_bias(split_residues):
    blk, sb = ATT_BLK, ATT_BLK * ATT_SUB
    halo = sb if split_residues else blk
    row = np.arange(sb)[:, None]
    col = np.arange(halo + sb + halo)[None, :]
    if split_residues:
        tile, a_k, j_k = col // sb, (col % sb) // blk, col % blk
        ok = (a_k == row // blk) & (np.abs(blk * (tile - 1) + j_k - row % blk) <= blk)
        has_prev, has_next = tile != 0, tile != 2
    else:
        ok = np.abs(col - halo - row) <= blk
        has_prev, has_next = col >= halo, col < halo + sb
    variants = []
    for v in range(4):
        okv = ok & (has_prev | (v & 1 == 0)) & (has_next | (v & 2 == 0))
        variants.append(np.where(okv, 0.0, NEG_BIG))
    return np.stack(variants).astype(np.float32)


def _attn_units(trunks, dil, split_residues):
    sb = ATT_BLK * ATT_SUB
    chunk = QKV_TM // dil
    cols = [[] for _ in range(6)]
    for base, B, S in trunks:
        assert base % QKV_TM == 0 and S % QKV_TM == 0
        L = S // dil
        nb = L // ATT_BLK
        for b in range(B):
            t0 = (base + b * S) // QKV_TM
            if split_residues:
                assert chunk == ATT_BLK and dil % ATT_SUB == 0
                for n in range(nb):
                    for rq in range(dil // ATT_SUB):
                        per = QKV_TM // sb
                        cur = (t0 + n) * per + rq
                        prev = (t0 + max(n - 1, 0)) * per + rq
                        nxt = (t0 + min(n + 1, nb - 1)) * per + rq
                        variant = (n == 0) + 2 * (n == nb - 1)
                        vals = (cur, prev, nxt, (base + b * S) // QKV_TM + n, rq, variant)
                        for cl, v in zip(cols, vals):
                            cl.append(v)
            else:
                assert chunk % sb == 0 and L % sb == 0
                for m in range(L // sb):
                    for r in range(dil):
                        def row_of(n):
                            n = min(max(n, 0), nb - 1)
                            tt = t0 + (ATT_BLK * n * dil) // QKV_TM
                            return tt * QKV_TM + r * chunk + (ATT_BLK * n) % chunk
                        n0 = m * ATT_SUB
                        variant = (n0 == 0) + 2 * (n0 + ATT_SUB == nb)
                        vals = (row_of(n0) // sb, row_of(n0 - 1) // ATT_BLK, row_of(n0 + ATT_SUB) // ATT_BLK,
                                (base + b * S) // (sb * dil) + m, r, variant)
                        for cl, v in zip(cols, vals):
                            cl.append(v)
    return [np.asarray(cl, np.int32) for cl in cols]


def _attention_group(qkv, dil, trunks):
    T = qkv.shape[0]
    sb = ATT_BLK * ATT_SUB
    split_residues = (QKV_TM // dil) == ATT_BLK
    meta = _attn_units(trunks, dil, split_residues)
    n_units = len(meta[0])
    halo = sb if split_residues else ATT_BLK
    out_rows = QKV_TM if split_residues else sb * dil

    def spec(rows, which, c):
        return pl.BlockSpec((rows, ATTN_WIDTH), lambda u, *refs: (refs[which][u], c))

    bias = jnp.asarray(_attn_bias(split_residues))
    kern = functools.partial(_attn_kernel, dil=dil, split_residues=split_residues)
    grid_spec = pltpu.PrefetchScalarGridSpec(
        num_scalar_prefetch=6,
        grid=(n_units,),
        in_specs=[spec(sb, 0, 0), spec(sb, 0, 1), spec(halo, 1, 1), spec(halo, 2, 1),
                  spec(sb, 0, 2), spec(halo, 1, 2), spec(halo, 2, 2),
                  pl.BlockSpec((1,) + bias.shape[1:], lambda u, *refs: (refs[5][u], 0, 0))],
        out_specs=(
            pl.BlockSpec((HEADS_PER_GROUP, out_rows, HEAD_DIM), lambda u, *refs: (0, refs[3][u], 0)),
            pl.BlockSpec((out_rows, LANES), lambda u, *refs: (refs[3][u], 0)),
        ),
    )
    return pl.pallas_call(
        kern,
        out_shape=(jax.ShapeDtypeStruct((HEADS_PER_GROUP, T, HEAD_DIM), F32),
                   jax.ShapeDtypeStruct((T, LANES), F32)),
        grid_spec=grid_spec,
        compiler_params=_cparams(("arbitrary",)),
        name=f"band_attn_d{dil}",
    )(*[jnp.asarray(m) for m in meta], qkv, qkv, qkv, qkv, qkv, qkv, qkv, bias)


def _combine_groups(o_refs, l_refs):
    ls = [l[...] for l in l_refs]
    m = functools.reduce(jnp.maximum, ls)
    es = [jnp.exp(l - m) for l in ls]
    inv = 1.0 / functools.reduce(lambda a, b: a + b, es)
    ws = [e * inv for e in es]
    parts = []
    for h in range(HEADS_PER_GROUP):
        acc = None
        for w, o in zip(ws, o_refs):
            term = w[:, h:h + 1] * o[h]
            acc = term if acc is None else acc + term
        parts.append(acc)
    return jnp.concatenate(parts, axis=1).astype(BF16)


def _route(h1, rw_ref, rb_ref, cnt_ref, route_ref, n_experts):
    tm = h1.shape[0]
    a_hi = h1.astype(BF16)
    a_lo = (h1 - a_hi.astype(F32)).astype(BF16)
    r1 = jnp.dot(a_hi, rw_ref[...], preferred_element_type=F32)
    r2 = jnp.dot(a_lo, rw_ref[...], preferred_element_type=F32)
    tot = r1 + r2
    logits = tot + pltpu.roll(tot, LANES - n_experts, 1) + rb_ref[...]
    lane = lax.broadcasted_iota(jnp.int32, (tm, LANES), 1)
    lg = jnp.where(lane < n_experts, logits, -jnp.inf)
    m1 = jnp.max(lg, axis=-1, keepdims=True)
    i1 = jnp.min(jnp.where(lg == m1, lane, LANES), axis=-1, keepdims=True)
    in_grp = ((lane // EXPERTS_PER_GROUP) == (i1 // EXPERTS_PER_GROUP)) & (lane != i1)
    lg2 = jnp.where(in_grp, lg, -jnp.inf)
    m2 = jnp.max(lg2, axis=-1, keepdims=True)
    i2 = jnp.min(jnp.where(lg2 == m2, lane, LANES), axis=-1, keepdims=True)
    e2 = jnp.exp(m2 - m1)
    g1 = 1.0 / (1.0 + e2)
    g2 = e2 / (1.0 + e2)
    oh1 = lane == i1
    oh2 = lane == i2
    ohs = jnp.where(oh1 | oh2, 1.0, 0.0)
    rr = lax.broadcasted_iota(jnp.int32, (tm, tm), 0)
    cc = lax.broadcasted_iota(jnp.int32, (tm, tm), 1)
    tri = jnp.where(cc < rr, 1.0, 0.0).astype(BF16)
    before = jnp.dot(tri, ohs.astype(BF16), preferred_element_type=F32) + cnt_ref[0:1, :]
    rank1 = jnp.sum(jnp.where(oh1, before, 0.0), axis=-1, keepdims=True)
    rank2 = jnp.sum(jnp.where(oh2, before, 0.0), axis=-1, keepdims=True)
    cnt_ref[...] = cnt_ref[...] + jnp.sum(ohs, axis=0, keepdims=True)
    rec = jnp.zeros((tm, LANES), F32)
    for ln, v in ((R_E1, i1.astype(F32)), (R_E2, i2.astype(F32)), (R_RANK1, rank1), (R_RANK2, rank2),
                  (R_G1, g1), (R_G2, g2)):
        rec = jnp.where(lane == ln, v, rec)
    route_ref[...] = rec


def _store_packed_rows(ref, x):
    tm, d = x.shape
    half = d // 2
    assert half == ROW_TILE * LANES
    lo = lax.bitcast_convert_type(x[:, :half].astype(BF16).astype(F32), U32)
    hi = lax.bitcast_convert_type(x[:, half:].astype(BF16).astype(F32), U32)
    packed = lax.shift_right_logical(lo, jnp.uint32(16)) | (hi & jnp.uint32(0xFFFF0000))
    for s in range(ROW_TILE):
        ref[pl.ds(s, tm, stride=ROW_TILE), :] = packed[:, s * LANES:(s + 1) * LANES]


def _load_packed_rows(ref, n_rows):
    los, his = [], []
    for s in range(ROW_TILE):
        w = ref[pl.ds(s, n_rows, stride=ROW_TILE), :]
        los.append(lax.bitcast_convert_type(lax.shift_left(w, jnp.uint32(16)), F32))
        his.append(lax.bitcast_convert_type(w & jnp.uint32(0xFFFF0000), F32))
    return jnp.concatenate(los + his, axis=1)


def _tail_kernel(*refs, n_mix, n_first, alpha, n_experts):
    mix_refs = refs[:n_mix]
    (hres_ref, w_ref, g_ref, b_ref, rw_ref, rb_ref, h_ref, hp_ref, route_ref, cnt_ref) = refs[n_mix:]

    @pl.when(pl.program_id(0) == 0)
    def _():
        cnt_ref[...] = jnp.zeros_like(cnt_ref)

    if n_mix == 2:
        mixin = jnp.where(pl.program_id(0) < n_first, mix_refs[0][...], mix_refs[1][...]).astype(BF16)
    else:
        mixin = _combine_groups(mix_refs[:N_DIL_GROUPS], mix_refs[N_DIL_GROUPS:])
    mix = jnp.dot(mixin, w_ref[...], preferred_element_type=F32)
    h1 = _layer_norm_rows(alpha * hres_ref[...] + mix, g_ref[...], b_ref[...])
    h_ref[...] = h1
    _store_packed_rows(hp_ref, h1)
    _route(h1, rw_ref, rb_ref, cnt_ref, route_ref, n_experts)


def _mixer_tail(mix_inputs, hres, w_bf16, g, b, rw_cat, rb_pad, alpha, n_experts):
    T, D = hres.shape
    n_mix = len(mix_inputs)
    tm = TOK_TM
    assert T % tm == 0
    kw = w_bf16.shape[0]
    n_first = None
    if n_mix == 2:
        assert mix_inputs[0].shape[0] % tm == 0
        n_first = mix_inputs[0].shape[0] // tm
        mix_specs = [pl.BlockSpec((tm, D), lambda i: (jnp.minimum(i, n_first - 1), 0)),
                     pl.BlockSpec((tm, D), lambda i: (jnp.maximum(i - n_first, 0), 0))]
    else:
        mix_specs = [pl.BlockSpec((tm, a.shape[1]), lambda i: (i, 0)) if a.ndim == 2
                     else pl.BlockSpec((a.shape[0], tm, a.shape[2]), lambda i: (0, i, 0)) for a in mix_inputs]
    kern = functools.partial(_tail_kernel, n_mix=n_mix, n_first=n_first, alpha=alpha, n_experts=n_experts)
    return pl.pallas_call(
        kern,
        out_shape=(jax.ShapeDtypeStruct((T, D), F32), jax.ShapeDtypeStruct((T * ROW_TILE, LANES), U32),
                   jax.ShapeDtypeStruct((T, LANES), F32), jax.ShapeDtypeStruct((SUBLANES, LANES), F32)),
        grid=(T // tm,),
        in_specs=mix_specs + [
            pl.BlockSpec((tm, D), lambda i: (i, 0)),
            pl.BlockSpec((kw, D), lambda i: (0, 0)),
            pl.BlockSpec((1, D), lambda i: (0, 0)),
            pl.BlockSpec((1, D), lambda i: (0, 0)),
            pl.BlockSpec((D, LANES), lambda i: (0, 0)),
            pl.BlockSpec((1, LANES), lambda i: (0, 0)),
        ],
        out_specs=(
            pl.BlockSpec((tm, D), lambda i: (i, 0)),
            pl.BlockSpec((tm * ROW_TILE, LANES), lambda i: (i, 0)),
            pl.BlockSpec((tm, LANES), lambda i: (i, 0)),
            pl.BlockSpec((SUBLANES, LANES), lambda i: (0, 0)),
        ),
        compiler_params=_cparams(("arbitrary",)),
        name="mixer_tail_router",
    )(*mix_inputs, hres, w_bf16, g, b, rw_cat, rb_pad)


def _plan_kernel(route_ref, cnt_ref, dest_ref, plan_ref, rows_ref, *, tm_e, n_experts):
    tm = route_ref.shape[0]
    lane8 = lax.broadcasted_iota(jnp.int32, (SUBLANES, LANES), 1)
    cnt = cnt_ref[...]
    padded = jnp.floor((cnt + (tm_e - 1)) * (1.0 / tm_e)) * tm_e
    pad_end = padded
    k = 1
    while k < LANES:
        pad_end = pad_end + jnp.where(lane8 >= k, pltpu.roll(pad_end, k, 1), 0.0)
        k *= 2
    pad_start = pad_end - padded

    rec = route_ref[...]
    lane = lax.broadcasted_iota(jnp.int32, (tm, LANES), 1)
    lanef = lane.astype(F32)
    ps = pad_start[0:1, :]
    d1 = jnp.sum(jnp.where(lanef == rec[:, R_E1:R_E1 + 1], ps, 0.0), axis=-1, keepdims=True) + rec[:, R_RANK1:R_RANK1 + 1]
    d2 = jnp.sum(jnp.where(lanef == rec[:, R_E2:R_E2 + 1], ps, 0.0), axis=-1, keepdims=True) + rec[:, R_RANK2:R_RANK2 + 1]
    both = jnp.where(lane == 0, d1, jnp.where(lane == 1, d2, 0.0))
    bt = both.T
    dest_ref[0] = jnp.concatenate([bt[0:1, :], bt[1:2, :]], axis=1).astype(jnp.int32)

    @pl.when(pl.program_id(0) == 0)
    def _():
        sub8 = lax.broadcasted_iota(jnp.int32, (SUBLANES, LANES), 0)
        tile_row = ((sub8 * LANES + lane8) * tm_e).astype(F32)
        te = jnp.zeros((SUBLANES, LANES), F32)
        for e in range(n_experts):
            te = te + jnp.where(pad_end[:, e:e + 1] <= tile_row, 1.0, 0.0)
        te = jnp.minimum(te, float(n_experts - 1))
        nv = pad_end[:, n_experts - 1:n_experts] * (1.0 / tm_e)
        plan_ref[...] = jnp.where(sub8 == PLAN_NV_ROW, nv, te).astype(jnp.int32)
        real_end = pad_start + cnt
        end_here = jnp.zeros((SUBLANES, LANES), F32)
        for e in range(n_experts):
            end_here = jnp.where(te == float(e), real_end[:, e:e + 1], end_here)
        rows_ref[...] = jnp.clip(end_here - tile_row, 0.0, float(tm_e)).astype(jnp.int32)


def _dispatch_plan(route, counts, n_experts, n_tiles):
    T = route.shape[0]
    tm = TOK_TM
    assert n_tiles <= PLAN_NV_ROW * LANES
    small = jax.ShapeDtypeStruct((SUBLANES, LANES), jnp.int32)
    small_spec = pl.BlockSpec((SUBLANES, LANES), lambda i: (0, 0))
    dest, plan, rows = pl.pallas_call(
        functools.partial(_plan_kernel, tm_e=EXP_TM, n_experts=n_experts),
        out_shape=(jax.ShapeDtypeStruct((T // tm, 1, 2 * tm), jnp.int32), small, small),
        grid=(T // tm,),
        in_specs=[pl.BlockSpec((tm, LANES), lambda i: (i, 0)), small_spec],
        out_specs=(pl.BlockSpec((1, 1, 2 * tm), lambda i: (i, 0, 0)), small_spec, small_spec),
        compiler_params=_cparams(("arbitrary",)),
        name="dispatch_plan",
    )(route, counts)
    tile_expert = plan[:PLAN_NV_ROW].reshape(-1)[:n_tiles]
    tile_rows = rows.reshape(-1)[:n_tiles]
    n_valid = plan[PLAN_NV_ROW, 0:1]
    return dest, tile_expert, tile_rows, n_valid


def _row_copy(src_ref, src_row, dst_ref, dst_row, sem):
    src = src_ref.at[pl.ds(pl.multiple_of(src_row * ROW_TILE, ROW_TILE), ROW_TILE)]
    dst = dst_ref.at[pl.ds(pl.multiple_of(dst_row * ROW_TILE, ROW_TILE), ROW_TILE)]
    return pltpu.make_async_copy(src, dst, sem)


def _dispatch_kernel(dest_ref, h_ref, xs_in_ref, xs_ref, sem):
    del xs_in_ref
    tm = h_ref.shape[0] // ROW_TILE

    def issue(r, carry):
        _row_copy(h_ref, r, xs_ref, dest_ref[0, 0, r], sem).start()
        _row_copy(h_ref, r, xs_ref, dest_ref[0, 0, tm + r], sem).start()
        return carry

    lax.fori_loop(0, tm, issue, 0)

    def drain(r, carry):
        _row_copy(h_ref, 0, xs_ref, 0, sem).wait()
        _row_copy(h_ref, 0, xs_ref, 0, sem).wait()
        return carry

    lax.fori_loop(0, tm, drain, 0)


def _dispatch(hp, dest_tiles, n_rows):
    T = hp.shape[0] // ROW_TILE
    tm = TOK_TM
    xs0 = jnp.zeros((n_rows * ROW_TILE, LANES), hp.dtype)
    return pl.pallas_call(
        _dispatch_kernel,
        out_shape=jax.ShapeDtypeStruct(xs0.shape, hp.dtype),
        grid=(T // tm,),
        in_specs=[
            pl.BlockSpec((1, 1, 2 * tm), lambda i: (i, 0, 0), memory_space=pltpu.SMEM),
            pl.BlockSpec((tm * ROW_TILE, LANES), lambda i: (i, 0)),
            pl.BlockSpec(memory_space=pl.ANY),
        ],
        out_specs=pl.BlockSpec(memory_space=pl.ANY),
        scratch_shapes=[pltpu.SemaphoreType.DMA(())],
        input_output_aliases={2: 0},
        compiler_params=_cparams(("arbitrary",)),
        name="expert_dispatch",
    )(dest_tiles, hp, xs0)


def _gmm_kernel(te_ref, nv_ref, tr_ref, *refs, mode, tm, packed_in, packed_out):
    if mode == "swiglu":
        x_ref, w_ref, a_ref, o_ref, wbf_ref = refs
    else:
        x_ref, w_ref, o_ref, wbf_ref = refs
    j = pl.program_id(0)
    out_rows_per_row = ROW_TILE if packed_out else 1

    def compute(rows):
        x = _load_packed_rows(x_ref, rows).astype(BF16) if packed_in else x_ref[:rows, :]
        acc = jnp.dot(x, wbf_ref[...], preferred_element_type=F32)
        if mode == "swiglu":
            a = a_ref[:rows, :].astype(F32)
            acc = a * (1.0 / (1.0 + jnp.exp(-a))) * acc
        if packed_out:
            _store_packed_rows(o_ref, acc)
        else:
            o_ref[:rows, :] = acc.astype(o_ref.dtype)
        if rows < tm:
            rest = pl.ds(rows * out_rows_per_row, (tm - rows) * out_rows_per_row)
            o_ref[rest, :] = jnp.zeros(((tm - rows) * out_rows_per_row, o_ref.shape[1]), o_ref.dtype)

    @pl.when(j < nv_ref[0])
    def _():
        prev = te_ref[jnp.maximum(j - 1, 0)]

        @pl.when((j == 0) | (te_ref[j] != prev))
        def _():
            wbf_ref[...] = w_ref[0, 0].astype(BF16)

        @pl.when(tr_ref[j] > tm // 2)
        def _():
            compute(tm)

        @pl.when(tr_ref[j] <= tm // 2)
        def _():
            compute(tm // 2)

    @pl.when(j >= nv_ref[0])
    def _():
        o_ref[...] = jnp.zeros_like(o_ref)


def _gmm(x, w, layer, tile_plan, n_tiles, a=None, packed_out=False):
    K, N = w.shape[-2:]
    packed_in = x.dtype == U32
    tm = EXP_TM
    mode = "plain" if a is None else "swiglu"

    def row_map(j, te, nv, tr):
        return (jnp.minimum(j, nv[0] - 1), 0)

    def w_map(j, te, nv, tr):
        return (layer, te[jnp.minimum(j, nv[0] - 1)], 0, 0)

    x_block = (tm * ROW_TILE, LANES) if packed_in else (tm, K)
    out_block = (tm * ROW_TILE, LANES) if packed_out else (tm, N)
    in_specs = [pl.BlockSpec(x_block, row_map), pl.BlockSpec((1, 1, K, N), w_map)]
    args = [x, w]
    if a is not None:
        in_specs.append(pl.BlockSpec((tm, N), row_map))
        args.append(a)
    grid_spec = pltpu.PrefetchScalarGridSpec(
        num_scalar_prefetch=3,
        grid=(n_tiles,),
        in_specs=in_specs,
        out_specs=pl.BlockSpec(out_block, lambda j, te, nv, tr: (j, 0)),
        scratch_shapes=[pltpu.VMEM((K, N), BF16)],
    )
    out_shape = (n_tiles * out_block[0], out_block[1])
    return pl.pallas_call(
        functools.partial(_gmm_kernel, mode=mode, tm=tm, packed_in=packed_in, packed_out=packed_out),
        out_shape=jax.ShapeDtypeStruct(out_shape, U32 if packed_out else BF16),
        grid_spec=grid_spec,
        compiler_params=_cparams(("arbitrary",)),
        name=f"expert_matmul_{mode}",
    )(*tile_plan, *args)


def _combine_kernel(dest_ref, next_ref, route_ref, hres_ref, y_ref, g_ref, b_ref, *refs, alpha, n_first):
    *out_refs, buf_ref, sems = refs
    o1_ref, o2_ref = out_refs[0], out_refs[-1]
    tm = hres_ref.shape[0]
    i = pl.program_id(0)
    n = pl.num_programs(0)

    def gather(d_ref, s):
        def issue(r, carry):
            _row_copy(y_ref, d_ref[0, 0, r], buf_ref.at[s, 0], r, sems.at[s]).start()
            _row_copy(y_ref, d_ref[0, 0, tm + r], buf_ref.at[s, 1], r, sems.at[s]).start()
            return carry
        lax.fori_loop(0, tm, issue, 0)

    def finish(s):
        def drain(r, carry):
            _row_copy(y_ref, 0, buf_ref.at[s, 0], 0, sems.at[s]).wait()
            _row_copy(y_ref, 0, buf_ref.at[s, 1], 0, sems.at[s]).wait()
            return carry

        lax.fori_loop(0, tm, drain, 0)
        rec = route_ref[...]
        f = (rec[:, R_G1:R_G1 + 1] * _load_packed_rows(buf_ref.at[s, 0], tm)
             + rec[:, R_G2:R_G2 + 1] * _load_packed_rows(buf_ref.at[s, 1], tm))
        h2 = _layer_norm_rows(alpha * hres_ref[...] + f, g_ref[...], b_ref[...])
        if n_first is None:
            o1_ref[...] = h2
        else:
            @pl.when(i < n_first)
            def _():
                o1_ref[...] = h2

            @pl.when(i >= n_first)
            def _():
                o2_ref[...] = h2

    @pl.when(i == 0)
    def _():
        gather(dest_ref, 0)

    for s in range(2):
        @pl.when(lax.rem(i, 2) == s)
        def _(s=s):
            @pl.when(i + 1 < n)
            def _():
                gather(next_ref, 1 - s)

            finish(s)


def _combine(dest_tiles, route, hres, y, g, b, alpha, split_rows=None):
    T, D = hres.shape
    tm = TOK_TM
    n_steps = T // tm
    if split_rows is None:
        n_first = None
        out_shape = (jax.ShapeDtypeStruct((T, D), F32),)
        out_specs = (pl.BlockSpec((tm, D), lambda i: (i, 0)),)
    else:
        assert split_rows % tm == 0
        n_first = split_rows // tm
        out_shape = (jax.ShapeDtypeStruct((split_rows, D), F32), jax.ShapeDtypeStruct((T - split_rows, D), F32))
        out_specs = (pl.BlockSpec((tm, D), lambda i: (jnp.minimum(i, n_first - 1), 0)),
                     pl.BlockSpec((tm, D), lambda i: (jnp.maximum(i - n_first, 0), 0)))
    return pl.pallas_call(
        functools.partial(_combine_kernel, alpha=alpha, n_first=n_first),
        out_shape=out_shape,
        grid=(n_steps,),
        in_specs=[
            pl.BlockSpec((1, 1, 2 * tm), lambda i: (i, 0, 0), memory_space=pltpu.SMEM),
            pl.BlockSpec((1, 1, 2 * tm), lambda i: (jnp.minimum(i + 1, n_steps - 1), 0, 0), memory_space=pltpu.SMEM),
            pl.BlockSpec((tm, LANES), lambda i: (i, 0)),
            pl.BlockSpec((tm, D), lambda i: (i, 0)),
            pl.BlockSpec(memory_space=pl.ANY),
            pl.BlockSpec((1, D), lambda i: (0, 0)),
            pl.BlockSpec((1, D), lambda i: (0, 0)),
        ],
        out_specs=out_specs,
        scratch_shapes=[pltpu.VMEM((2, 2, tm * ROW_TILE, LANES), U32), pltpu.SemaphoreType.DMA((2,))],
        compiler_params=_cparams(("arbitrary",)),
        name="expert_combine_ln",
    )(dest_tiles, dest_tiles, route, hres, y, g, b)


def _grouped_moe(h, hp, route, counts, w_gate, w_up, w_down, layer, g, b, alpha, split_rows=None):
    T, D = h.shape
    E = w_gate.shape[1]
    n_tiles = (T * 2) // EXP_TM + E
    dest_tiles, tile_expert, tile_rows, n_valid = _dispatch_plan(route, counts, E, n_tiles)
    tile_plan = (tile_expert, n_valid, tile_rows)
    xs = _dispatch(hp, dest_tiles, n_tiles * EXP_TM)
    a = _gmm(xs, w_gate, layer, tile_plan, n_tiles)
    hid = _gmm(xs, w_up, layer, tile_plan, n_tiles, a=a)
    y = _gmm(hid, w_down, layer, tile_plan, n_tiles, packed_out=True)
    return _combine(dest_tiles, route, h, y, g, b, alpha, split_rows)


def _fourier_a_kernel(x_ref, cs_ref, f_ref, y_ref, *, gd):
    n1 = x_ref.shape[0]
    for jj in range(SEQ_SUB):
        x = x_ref[:, jj, :].astype(BF16)
        zr, zi = [], []
        for gi in range(FOURIER_GROUPS):
            pq = jnp.dot(x[:, gi * gd:(gi + 1) * gd], cs_ref[...], preferred_element_type=F32)
            zr.append(pq[:, :gd])
            zi.append(pq[:, gd:])
        z = jnp.concatenate([jnp.concatenate(zr, axis=1), jnp.concatenate(zi, axis=1)], axis=0).astype(BF16)
        y = jnp.dot(f_ref[jj], z, preferred_element_type=F32)
        re = lax.bitcast_convert_type(y[:n1].astype(BF16).astype(F32), U32)
        im = lax.bitcast_convert_type(y[n1:].astype(BF16).astype(F32), U32)
        y_ref[:, jj, :] = lax.shift_right_logical(re, jnp.uint32(16)) | (im & jnp.uint32(0xFFFF0000))


def _fourier_b_kernel(y_ref, f_ref, o_ref, *, norm):
    n2 = o_ref.shape[0]
    for jj in range(SEQ_SUB):
        w = y_ref[jj * n2:(jj + 1) * n2, :]
        yr = lax.bitcast_convert_type(lax.shift_left(w, jnp.uint32(16)), F32)
        yi = lax.bitcast_convert_type(w & jnp.uint32(0xFFFF0000), F32)
        y = jnp.concatenate([yr, yi], axis=0).astype(BF16)
        o_ref[:, jj, :] = jnp.dot(f_ref[...], y, preferred_element_type=F32) * norm


def _dft_tables(S, gd):
    n1, n2 = S // SEQ_N2, SEQ_N2
    jc = (jnp.arange(gd)[:, None] * jnp.arange(gd)[None, :]) % gd
    ang = jc.astype(F32) * (2.0 * math.pi / gd)
    cs = jnp.concatenate([jnp.cos(ang), -jnp.sin(ang)], axis=1).astype(BF16)
    k1 = jnp.arange(n1)[None, :, None]
    s1 = jnp.arange(n1)[None, None, :]
    s2 = jnp.arange(n2)[:, None, None]
    th = ((k1 * (n2 * s1 + s2)) % S).astype(F32) * (2.0 * math.pi / S)
    fr, fi = jnp.cos(th), -jnp.sin(th)
    f1 = jnp.concatenate([jnp.concatenate([fr, -fi], axis=2), jnp.concatenate([fi, fr], axis=2)], axis=1).astype(BF16)
    kk = (jnp.arange(n2)[:, None] * jnp.arange(n2)[None, :]) % n2
    th2 = kk.astype(F32) * (2.0 * math.pi / n2)
    f2 = jnp.concatenate([jnp.cos(th2), jnp.sin(th2)], axis=1).astype(BF16)
    return cs, f1, f2


def _fourier_mixed(h, base, B, S):
    T, D = h.shape
    gd = D // FOURIER_GROUPS
    n1, n2 = S // SEQ_N2, SEQ_N2
    assert n1 % SEQ_SUB == 0 and n2 % SEQ_SUB == 0 and base % S == 0
    cs, f1, f2 = _dft_tables(S, gd)
    off = base // S
    y = pl.pallas_call(
        functools.partial(_fourier_a_kernel, gd=gd),
        out_shape=jax.ShapeDtypeStruct((B * n1, n2, D), U32),
        grid=(B, n2 // SEQ_SUB),
        in_specs=[
            pl.BlockSpec((n1, SEQ_SUB, D), lambda b, s: (off + b, s, 0)),
            pl.BlockSpec((gd, 2 * gd), lambda b, s: (0, 0)),
            pl.BlockSpec((SEQ_SUB, 2 * n1, 2 * n1), lambda b, s: (s, 0, 0)),
        ],
        out_specs=pl.BlockSpec((n1, SEQ_SUB, D), lambda b, s: (b, s, 0)),
        compiler_params=_cparams(("arbitrary", "arbitrary")),
        name="fourier_channel_step1",
    )(h.reshape(T // n2, n2, D), cs, f1)
    mixed = pl.pallas_call(
        functools.partial(_fourier_b_kernel, norm=1.0 / math.sqrt(S * gd)),
        out_shape=jax.ShapeDtypeStruct((B * n2, n1, D), F32),
        grid=(B, n1 // SEQ_SUB),
        in_specs=[
            pl.BlockSpec((SEQ_SUB * n2, D), lambda b, k: (b * (n1 // SEQ_SUB) + k, 0)),
            pl.BlockSpec((n2, 2 * n2), lambda b, k: (0, 0)),
        ],
        out_specs=pl.BlockSpec((n2, SEQ_SUB, D), lambda b, k: (b, k, 0)),
        compiler_params=_cparams(("arbitrary", "arbitrary")),
        name="fourier_step2",
    )(y.reshape(B * S, D), f2)
    return mixed.reshape(B * S, D)


def _reorder_qk_columns(w):
    D = w.shape[0]
    nblk = HEAD_DIM // ROT_HALF
    gap = ROT_PARTNER // ROT_HALF
    order = [0] + list(range(2, gap + 1)) + [1] + list(range(gap + 1, nblk))
    w6 = w.reshape(D, N_DIL_GROUPS, 3, HEADS_PER_GROUP, nblk, ROT_HALF)
    qk = w6[:, :, :2]
    qk = jnp.concatenate([qk[:, :, :, :, i:i + 1] for i in order], axis=4)
    return jnp.concatenate([qk, w6[:, :, 2:]], axis=2).reshape(w.shape)


def _rotary_tables(trunks, T):
    inv_freq = ROPE_THETA ** (-jnp.arange(0, ROT_DIM, 2, dtype=F32) / ROT_DIM)
    pos = jnp.concatenate([jnp.tile(jnp.arange(S, dtype=F32), B) for _, B, S in trunks])
    ang = pos[:, None] * inv_freq[None, :]
    cos, sin = jnp.cos(ang), jnp.sin(ang)
    gap = ROT_PARTNER - ROT_HALF
    cos_t = jnp.concatenate([cos, jnp.ones((T, gap), F32), cos, jnp.ones((T, gap), F32)], axis=1)
    sin_t = jnp.concatenate([-sin, jnp.zeros((T, gap), F32), sin, jnp.zeros((T, gap), F32)], axis=1)
    q_scale = HEAD_DIM ** -0.5
    cos3 = jnp.stack([cos_t * q_scale, cos_t, jnp.ones_like(cos_t)])
    sin3 = jnp.stack([sin_t * q_scale, sin_t, jnp.zeros_like(sin_t)])
    return cos3, sin3


def kernel(x_prompt, x_sample, ln_in_g, ln_in_b, attn_w_in, attn_w_out, fourier_w_out, router_w, router_b,
           expert_w_gate, expert_w_up, expert_w_down, ln_g, ln_b):
    Bp, Sp, D = x_prompt.shape
    Bs, Ss, _ = x_sample.shape
    depth = ln_g.shape[0]
    E = router_w.shape[1]
    alpha = (2 * depth) ** 0.25
    trunks = ((0, Bp, Sp), (Bp * Sp, Bs, Ss))
    T = Bp * Sp + Bs * Ss
    assert 2 * E == LANES and T % QKV_TM == 0
    assert depth == 2, "layer 0 (attention) is fused with the input LayerNorm, layer 1 is the Fourier layer"

    row = lambda v: v.reshape(1, -1).astype(F32)
    rw_hi = router_w.astype(BF16)
    rw_lo = (router_w - rw_hi.astype(F32)).astype(BF16)
    rw_cat = jnp.concatenate([rw_hi, rw_lo], axis=1)
    rb_pad = jnp.concatenate([router_b.astype(F32), jnp.zeros((LANES - E,), F32)]).reshape(1, LANES)
    cos3, sin3 = _rotary_tables(trunks, T)

    h, xn = _ln_in(x_prompt.reshape(Bp * Sp, D), x_sample.reshape(Bs * Ss, D), row(ln_in_g), row(ln_in_b))
    w_in = _reorder_qk_columns(attn_w_in[0]).astype(BF16)
    outs, lses = [], []
    for g, (_, dil) in enumerate(DILATION_PATTERNS):
        o, l = _attention_group(_qkv_group(xn, w_in, g, dil, cos3, sin3), dil, trunks)
        outs.append(o)
        lses.append(l)
    h, hp, route, counts = _mixer_tail(outs + lses, h, attn_w_out[0].astype(BF16),
                                       row(ln_g[0, 0]), row(ln_b[0, 0]), rw_cat, rb_pad, alpha, E)
    (h,) = _grouped_moe(h, hp, route, counts, expert_w_gate, expert_w_up, expert_w_down, 0,
                        row(ln_g[0, 1]), row(ln_b[0, 1]), alpha)

    mixed = [_fourier_mixed(h, base, B, S) for base, B, S in trunks]
    h, hp, route, counts = _mixer_tail(mixed, h, fourier_w_out[0].astype(BF16),
                                       row(ln_g[1, 0]), row(ln_b[1, 0]), rw_cat, rb_pad, alpha, E)
    y_p, y_s = _grouped_moe(h, hp, route, counts, expert_w_gate, expert_w_up, expert_w_down, 1,
                            row(ln_g[1, 1]), row(ln_b[1, 1]), alpha, split_rows=Bp * Sp)
    return (y_p.reshape(Bp, Sp, D), y_s.reshape(Bs, Ss, D))
```

```python
import functools
import math

import numpy as np
import jax
import jax.numpy as jnp
from jax import lax
from jax.experimental import pallas as pl
from jax.experimental.pallas import tpu as pltpu

F32 = jnp.float32
BF16 = jnp.bfloat16
U32 = jnp.uint32

DILATION_PATTERNS = ((128, 1), (512, 4), (2048, 16))
N_DIL_GROUPS = len(DILATION_PATTERNS)
HEAD_DIM = 128
HEADS_PER_GROUP = 8
ATTN_WIDTH = HEADS_PER_GROUP * HEAD_DIM
ROT_DIM = HEAD_DIM // 4
ROT_HALF = ROT_DIM // 2
ROT_PARTNER = HEAD_DIM // 2
ROPE_THETA = 500000.0
FOURIER_GROUPS = 8
EXPERTS_PER_GROUP = 8
LN_EPS = 1e-5
NEG_BIG = -1e30

LANES = 128
SUBLANES = 8
VMEM_LIMIT_BYTES = 56 * 1024 * 1024

QKV_TM = 1024
QKV_TN = ATTN_WIDTH
ATT_BLK = 64
ATT_SUB = 4
LN_TM = 512
ROW_TILE = SUBLANES
TOK_TM = 256
EXP_TM = 512
SEQ_SUB = SUBLANES
SEQ_N2 = 128

R_E1, R_E2, R_RANK1, R_RANK2, R_G1, R_G2 = 0, 1, 2, 3, 4, 5
PLAN_NV_ROW = SUBLANES - 1


def _cparams(sem, vmem=VMEM_LIMIT_BYTES):
    return pltpu.CompilerParams(dimension_semantics=sem, vmem_limit_bytes=vmem)


def _layer_norm_rows(x, g, b):
    mu = jnp.mean(x, axis=-1, keepdims=True)
    xc = x - mu
    var = jnp.mean(xc * xc, axis=-1, keepdims=True)
    return xc * lax.rsqrt(var + LN_EPS) * g + b


def _ln_in_kernel(xp_ref, xs_ref, g_ref, b_ref, h_ref, xn_ref, *, n_first):
    x = jnp.where(pl.program_id(0) < n_first, xp_ref[...], xs_ref[...])
    y = _layer_norm_rows(x, g_ref[...], b_ref[...])
    h_ref[...] = y
    xn_ref[...] = y.astype(BF16)


def _ln_in(xp, xs, g, b):
    Tp, D = xp.shape
    Ts = xs.shape[0]
    tm = LN_TM
    assert Tp % tm == 0 and Ts % tm == 0
    n_first = Tp // tm
    T = Tp + Ts
    return pl.pallas_call(
        functools.partial(_ln_in_kernel, n_first=n_first),
        out_shape=(jax.ShapeDtypeStruct((T, D), F32), jax.ShapeDtypeStruct((T, D), BF16)),
        grid=(T // tm,),
        in_specs=[
            pl.BlockSpec((tm, D), lambda i: (jnp.minimum(i, n_first - 1), 0)),
            pl.BlockSpec((tm, D), lambda i: (jnp.maximum(i - n_first, 0), 0)),
            pl.BlockSpec((1, D), lambda i: (0, 0)),
            pl.BlockSpec((1, D), lambda i: (0, 0)),
        ],
        out_specs=(pl.BlockSpec((tm, D), lambda i: (i, 0)), pl.BlockSpec((tm, D), lambda i: (i, 0))),
        compiler_params=_cparams(("arbitrary",)),
        name="ln_in",
    )(xp, xs, g, b)


def _qkv_kernel(x_ref, w_ref, c_ref, s_ref, o_ref, perm_ref, *, dil):
    acc = jnp.dot(x_ref[...], w_ref[...], preferred_element_type=F32)
    tm, tn = acc.shape
    chunk = tm // dil
    for cb in range(tn // LANES):
        ls = slice(cb * LANES, (cb + 1) * LANES)
        val = acc[:, ls]
        val = val * c_ref[0] + pltpu.roll(val, ROT_PARTNER, 1) * s_ref[0]
        if dil == 1:
            o_ref[:, ls] = val.astype(BF16)
        else:
            perm_ref[cb] = val
            for r in range(dil):
                o_ref[r * chunk:(r + 1) * chunk, ls] = perm_ref[cb, pl.ds(r, chunk, stride=dil), :].astype(BF16)


def _qkv_group(xn, w_bf16, g, dil, cos3, sin3):
    T, D = xn.shape
    tm, tn = QKV_TM, QKV_TN
    assert T % tm == 0
    return pl.pallas_call(
        functools.partial(_qkv_kernel, dil=dil),
        out_shape=jax.ShapeDtypeStruct((T, 3 * tn), BF16),
        grid=(T // tm, 3),
        in_specs=[
            pl.BlockSpec((tm, D), lambda i, c: (i, 0)),
            pl.BlockSpec((D, tn), lambda i, c: (0, g * 3 + c)),
            pl.BlockSpec((1, tm, LANES), lambda i, c: (c, i, 0)),
            pl.BlockSpec((1, tm, LANES), lambda i, c: (c, i, 0)),
        ],
        out_specs=pl.BlockSpec((tm, tn), lambda i, c: (i, c)),
        scratch_shapes=[pltpu.VMEM((tn // LANES, tm, LANES), F32)],
        compiler_params=_cparams(("arbitrary", "arbitrary")),
        name=f"qkv_proj_d{dil}",
    )(xn, w_bf16, cos3, sin3)


def _attn_kernel(qb_ref, kp_ref, kn_ref, ob_ref, rr_ref, var_ref,
                 q_ref, kc_ref, kp_blk, kn_blk, vc_ref, vp_blk, vn_blk, bias_ref, o_ref, l_ref,
                 *, dil, split_residues):
    u = pl.program_id(0)
    rr = rr_ref[u]
    blk = ATT_BLK
    sb = blk * ATT_SUB
    lane = lax.broadcasted_iota(jnp.int32, (sb, LANES), 1)
    bias = bias_ref[0]
    lse_tile = jnp.zeros((sb, LANES), F32)

    def store(ref, lead, val):
        if dil == 1:
            ref[lead + (slice(None), slice(None))] = val
        elif split_residues:
            for a in range(ATT_SUB):
                dst = pl.ds(rr * ATT_SUB + a, blk, stride=dil)
                ref[lead + (dst, slice(None))] = val[a * blk:(a + 1) * blk]
        else:
            ref[lead + (pl.ds(rr, sb, stride=dil), slice(None))] = val

    for h in range(HEADS_PER_GROUP):
        hs = slice(h * HEAD_DIM, (h + 1) * HEAD_DIM)
        kh = jnp.concatenate([kp_blk[:, hs], kc_ref[:, hs], kn_blk[:, hs]], axis=0)
        vh = jnp.concatenate([vp_blk[:, hs], vc_ref[:, hs], vn_blk[:, hs]], axis=0)
        s = lax.dot_general(q_ref[:, hs], kh, (((1,), (1,)), ((), ())), preferred_element_type=F32) + bias
        m = jnp.max(s, axis=-1, keepdims=True)
        p = jnp.exp(s - m)
        den = jnp.sum(p, axis=-1, keepdims=True)
        o = jnp.dot(p.astype(BF16), vh, preferred_element_type=F32) / den
        lse_tile = jnp.where(lane == h, m + jnp.log(den), lse_tile)
        store(o_ref, (h,), o)
    store(l_ref, (), lse_tile)


def _attn_bias(split_residues):
    blk, sb = ATT_BLK, ATT_BLK * ATT_SUB
    halo = sb if split_residues else blk
    row = np.arange(sb)[:, None]
    col = np.arange(halo + sb + halo)[None, :]
    if split_residues:
        tile, a_k, j_k = col // sb, (col % sb) // blk, col % blk
        ok = (a_k == row // blk) & (np.abs(blk * (tile - 1) + j_k - row % blk) <= blk)
        has_prev, has_next = tile != 0, tile != 2
    else:
        ok = np.abs(col - halo - row) <= blk
        has_prev, has_next = col >= halo, col < halo + sb
    variants = []
    for v in range(4):
        okv = ok & (has_prev | (v & 1 == 0)) & (has_next | (v & 2 == 0))
        variants.append(np.where(okv, 0.0, NEG_BIG))
    return np.stack(variants).astype(np.float32)


def _attn_units(trunks, dil, split_residues):
    sb = ATT_BLK * ATT_SUB
    chunk = QKV_TM // dil
    cols = [[] for _ in range(6)]
    for base, B, S in trunks:
        assert base % QKV_TM == 0 and S % QKV_TM == 0
        L = S // dil
        nb = L // ATT_BLK
        for b in range(B):
            t0 = (base + b * S) // QKV_TM
            if split_residues:
                assert chunk == ATT_BLK and dil % ATT_SUB == 0
                for n in range(nb):
                    for rq in range(dil // ATT_SUB):
                        per = QKV_TM // sb
                        cur = (t0 + n) * per + rq
                        prev = (t0 + max(n - 1, 0)) * per + rq
                        nxt = (t0 + min(n + 1, nb - 1)) * per + rq
                        variant = (n == 0) + 2 * (n == nb - 1)
                        vals = (cur, prev, nxt, (base + b * S) // QKV_TM + n, rq, variant)
                        for cl, v in zip(cols, vals):
                            cl.append(v)
            else:
                assert chunk % sb == 0 and L % sb == 0
                for m in range(L // sb):
                    for r in range(dil):
                        def row_of(n):
                            n = min(max(n, 0), nb - 1)
                            tt = t0 + (ATT_BLK * n * dil) // QKV_TM
                            return tt * QKV_TM + r * chunk + (ATT_BLK * n) % chunk
                        n0 = m * ATT_SUB
                        variant = (n0 == 0) + 2 * (n0 + ATT_SUB == nb)
                        vals = (row_of(n0) // sb, row_of(n0 - 1) // ATT_BLK, row_of(n0 + ATT_SUB) // ATT_BLK,
                                (base + b * S) // (sb * dil) + m, r, variant)
                        for cl, v in zip(cols, vals):
                            cl.append(v)
    return [np.asarray(cl, np.int32) for cl in cols]


def _attention_group(qkv, dil, trunks):
    T = qkv.shape[0]
    sb = ATT_BLK * ATT_SUB
    split_residues = (QKV_TM // dil) == ATT_BLK
    meta = _attn_units(trunks, dil, split_residues)
    n_units = len(meta[0])
    halo = sb if split_residues else ATT_BLK
    out_rows = QKV_TM if split_residues else sb * dil

    def spec(rows, which, c):
        return pl.BlockSpec((rows, ATTN_WIDTH), lambda u, *refs: (refs[which][u], c))

    bias = jnp.asarray(_attn_bias(split_residues))
    kern = functools.partial(_attn_kernel, dil=dil, split_residues=split_residues)
    grid_spec = pltpu.PrefetchScalarGridSpec(
        num_scalar_prefetch=6,
        grid=(n_units,),
        in_specs=[spec(sb, 0, 0), spec(sb, 0, 1), spec(halo, 1, 1), spec(halo, 2, 1),
                  spec(sb, 0, 2), spec(halo, 1, 2), spec(halo, 2, 2),
                  pl.BlockSpec((1,) + bias.shape[1:], lambda u, *refs: (refs[5][u], 0, 0))],
        out_specs=(
            pl.BlockSpec((HEADS_PER_GROUP, out_rows, HEAD_DIM), lambda u, *refs: (0, refs[3][u], 0)),
            pl.BlockSpec((out_rows, LANES), lambda u, *refs: (refs[3][u], 0)),
        ),
    )
    return pl.pallas_call(
        kern,
        out_shape=(jax.ShapeDtypeStruct((HEADS_PER_GROUP, T, HEAD_DIM), F32),
                   jax.ShapeDtypeStruct((T, LANES), F32)),
        grid_spec=grid_spec,
        compiler_params=_cparams(("arbitrary",)),
        name=f"band_attn_d{dil}",
    )(*[jnp.asarray(m) for m in meta], qkv, qkv, qkv, qkv, qkv, qkv, qkv, bias)


def _combine_groups(o_refs, l_refs):
    ls = [l[...] for l in l_refs]
    m = functools.reduce(jnp.maximum, ls)
    es = [jnp.exp(l - m) for l in ls]
    inv = 1.0 / functools.reduce(lambda a, b: a + b, es)
    ws = [e * inv for e in es]
    parts = []
    for h in range(HEADS_PER_GROUP):
        acc = None
        for w, o in zip(ws, o_refs):
            term = w[:, h:h + 1] * o[h]
            acc = term if acc is None else acc + term
        parts.append(acc)
    return jnp.concatenate(parts, axis=1).astype(BF16)


def _route(h1, rw_ref, rb_ref, cnt_ref, route_ref, n_experts):
    tm = h1.shape[0]
    a_hi = h1.astype(BF16)
    a_lo = (h1 - a_hi.astype(F32)).astype(BF16)
    r1 = jnp.dot(a_hi, rw_ref[...], preferred_element_type=F32)
    r2 = jnp.dot(a_lo, rw_ref[...], preferred_element_type=F32)
    tot = r1 + r2
    logits = tot + pltpu.roll(tot, LANES - n_experts, 1) + rb_ref[...]
    lane = lax.broadcasted_iota(jnp.int32, (tm, LANES), 1)
    lg = jnp.where(lane < n_experts, logits, -jnp.inf)
    m1 = jnp.max(lg, axis=-1, keepdims=True)
    i1 = jnp.min(jnp.where(lg == m1, lane, LANES), axis=-1, keepdims=True)
    in_grp = ((lane // EXPERTS_PER_GROUP) == (i1 // EXPERTS_PER_GROUP)) & (lane != i1)
    lg2 = jnp.where(in_grp, lg, -jnp.inf)
    m2 = jnp.max(lg2, axis=-1, keepdims=True)
    i2 = jnp.min(jnp.where(lg2 == m2, lane, LANES), axis=-1, keepdims=True)
    e2 = jnp.exp(m2 - m1)
    g1 = 1.0 / (1.0 + e2)
    g2 = e2 / (1.0 + e2)
    oh1 = lane == i1
    oh2 = lane == i2
    ohs = jnp.where(oh1 | oh2, 1.0, 0.0)
    rr = lax.broadcasted_iota(jnp.int32, (tm, tm), 0)
    cc = lax.broadcasted_iota(jnp.int32, (tm, tm), 1)
    tri = jnp.where(cc < rr, 1.0, 0.0).astype(BF16)
    before = jnp.dot(tri, ohs.astype(BF16), preferred_element_type=F32) + cnt_ref[0:1, :]
    rank1 = jnp.sum(jnp.where(oh1, before, 0.0), axis=-1, keepdims=True)
    rank2 = jnp.sum(jnp.where(oh2, before, 0.0), axis=-1, keepdims=True)
    cnt_ref[...] = cnt_ref[...] + jnp.sum(ohs, axis=0, keepdims=True)
    rec = jnp.zeros((tm, LANES), F32)
    for ln, v in ((R_E1, i1.astype(F32)), (R_E2, i2.astype(F32)), (R_RANK1, rank1), (R_RANK2, rank2),
                  (R_G1, g1), (R_G2, g2)):
        rec = jnp.where(lane == ln, v, rec)
    route_ref[...] = rec


def _store_packed_rows(ref, x):
    tm, d = x.shape
    half = d // 2
    assert half == ROW_TILE * LANES
    lo = lax.bitcast_convert_type(x[:, :half].astype(BF16).astype(F32), U32)
    hi = lax.bitcast_convert_type(x[:, half:].astype(BF16).astype(F32), U32)
    packed = lax.shift_right_logical(lo, jnp.uint32(16)) | (hi & jnp.uint32(0xFFFF0000))
    for s in range(ROW_TILE):
        ref[pl.ds(s, tm, stride=ROW_TILE), :] = packed[:, s * LANES:(s + 1) * LANES]


def _load_packed_rows(ref, n_rows):
    los, his = [], []
    for s in range(ROW_TILE):
        w = ref[pl.ds(s, n_rows, stride=ROW_TILE), :]
        los.append(lax.bitcast_convert_type(lax.shift_left(w, jnp.uint32(16)), F32))
        his.append(lax.bitcast_convert_type(w & jnp.uint32(0xFFFF0000), F32))
    return jnp.concatenate(los + his, axis=1)


def _tail_kernel(*refs, n_mix, n_first, alpha, n_experts):
    mix_refs = refs[:n_mix]
    (hres_ref, w_ref, g_ref, b_ref, rw_ref, rb_ref, h_ref, hp_ref, route_ref, cnt_ref) = refs[n_mix:]

    @pl.when(pl.program_id(0) == 0)
    def _():
        cnt_ref[...] = jnp.zeros_like(cnt_ref)

    if n_mix == 2:
        mixin = jnp.where(pl.program_id(0) < n_first, mix_refs[0][...], mix_refs[1][...]).astype(BF16)
    else:
        mixin = _combine_groups(mix_refs[:N_DIL_GROUPS], mix_refs[N_DIL_GROUPS:])
    mix = jnp.dot(mixin, w_ref[...], preferred_element_type=F32)
    h1 = _layer_norm_rows(alpha * hres_ref[...] + mix, g_ref[...], b_ref[...])
    h_ref[...] = h1
    _store_packed_rows(hp_ref, h1)
    _route(h1, rw_ref, rb_ref, cnt_ref, route_ref, n_experts)


def _mixer_tail(mix_inputs, hres, w_bf16, g, b, rw_cat, rb_pad, alpha, n_experts):
    T, D = hres.shape
    n_mix = len(mix_inputs)
    tm = TOK_TM
    assert T % tm == 0
    kw = w_bf16.shape[0]
    n_first = None
    if n_mix == 2:
        assert mix_inputs[0].shape[0] % tm == 0
        n_first = mix_inputs[0].shape[0] // tm
        mix_specs = [pl.BlockSpec((tm, D), lambda i: (jnp.minimum(i, n_first - 1), 0)),
                     pl.BlockSpec((tm, D), lambda i: (jnp.maximum(i - n_first, 0), 0))]
    else:
        mix_specs = [pl.BlockSpec((tm, a.shape[1]), lambda i: (i, 0)) if a.ndim == 2
                     else pl.BlockSpec((a.shape[0], tm, a.shape[2]), lambda i: (0, i, 0)) for a in mix_inputs]
    kern = functools.partial(_tail_kernel, n_mix=n_mix, n_first=n_first, alpha=alpha, n_experts=n_experts)
    return pl.pallas_call(
        kern,
        out_shape=(jax.ShapeDtypeStruct((T, D), F32), jax.ShapeDtypeStruct((T * ROW_TILE, LANES), U32),
                   jax.ShapeDtypeStruct((T, LANES), F32), jax.ShapeDtypeStruct((SUBLANES, LANES), F32)),
        grid=(T // tm,),
        in_specs=mix_specs + [
            pl.BlockSpec((tm, D), lambda i: (i, 0)),
            pl.BlockSpec((kw, D), lambda i: (0, 0)),
            pl.BlockSpec((1, D), lambda i: (0, 0)),
            pl.BlockSpec((1, D), lambda i: (0, 0)),
            pl.BlockSpec((D, LANES), lambda i: (0, 0)),
            pl.BlockSpec((1, LANES), lambda i: (0, 0)),
        ],
        out_specs=(
            pl.BlockSpec((tm, D), lambda i: (i, 0)),
            pl.BlockSpec((tm * ROW_TILE, LANES), lambda i: (i, 0)),
            pl.BlockSpec((tm, LANES), lambda i: (i, 0)),
            pl.BlockSpec((SUBLANES, LANES), lambda i: (0, 0)),
        ),
        compiler_params=_cparams(("arbitrary",)),
        name="mixer_tail_router",
    )(*mix_inputs, hres, w_bf16, g, b, rw_cat, rb_pad)


def _plan_kernel(route_ref, cnt_ref, dest_ref, plan_ref, rows_ref, *, tm_e, n_experts):
    tm = route_ref.shape[0]
    lane8 = lax.broadcasted_iota(jnp.int32, (SUBLANES, LANES), 1)
    cnt = cnt_ref[...]
    padded = jnp.floor((cnt + (tm_e - 1)) * (1.0 / tm_e)) * tm_e
    pad_end = padded
    k = 1
    while k < LANES:
        pad_end = pad_end + jnp.where(lane8 >= k, pltpu.roll(pad_end, k, 1), 0.0)
        k *= 2
    pad_start = pad_end - padded

    rec = route_ref[...]
    lane = lax.broadcasted_iota(jnp.int32, (tm, LANES), 1)
    lanef = lane.astype(F32)
    ps = pad_start[0:1, :]
    d1 = jnp.sum(jnp.where(lanef == rec[:, R_E1:R_E1 + 1], ps, 0.0), axis=-1, keepdims=True) + rec[:, R_RANK1:R_RANK1 + 1]
    d2 = jnp.sum(jnp.where(lanef == rec[:, R_E2:R_E2 + 1], ps, 0.0), axis=-1, keepdims=True) + rec[:, R_RANK2:R_RANK2 + 1]
    both = jnp.where(lane == 0, d1, jnp.where(lane == 1, d2, 0.0))
    bt = both.T
    dest_ref[0] = jnp.concatenate([bt[0:1, :], bt[1:2, :]], axis=1).astype(jnp.int32)

    @pl.when(pl.program_id(0) == 0)
    def _():
        sub8 = lax.broadcasted_iota(jnp.int32, (SUBLANES, LANES), 0)
        tile_row = ((sub8 * LANES + lane8) * tm_e).astype(F32)
        te = jnp.zeros((SUBLANES, LANES), F32)
        for e in range(n_experts):
            te = te + jnp.where(pad_end[:, e:e + 1] <= tile_row, 1.0, 0.0)
        te = jnp.minimum(te, float(n_experts - 1))
        nv = pad_end[:, n_experts - 1:n_experts] * (1.0 / tm_e)
        plan_ref[...] = jnp.where(sub8 == PLAN_NV_ROW, nv, te).astype(jnp.int32)
        real_end = pad_start + cnt
        end_here = jnp.zeros((SUBLANES, LANES), F32)
        for e in range(n_experts):
            end_here = jnp.where(te == float(e), real_end[:, e:e + 1], end_here)
        rows_ref[...] = jnp.clip(end_here - tile_row, 0.0, float(tm_e)).astype(jnp.int32)


def _dispatch_plan(route, counts, n_experts, n_tiles):
    T = route.shape[0]
    tm = TOK_TM
    assert n_tiles <= PLAN_NV_ROW * LANES
    small = jax.ShapeDtypeStruct((SUBLANES, LANES), jnp.int32)
    small_spec = pl.BlockSpec((SUBLANES, LANES), lambda i: (0, 0))
    dest, plan, rows = pl.pallas_call(
        functools.partial(_plan_kernel, tm_e=EXP_TM, n_experts=n_experts),
        out_shape=(jax.ShapeDtypeStruct((T // tm, 1, 2 * tm), jnp.int32), small, small),
        grid=(T // tm,),
        in_specs=[pl.BlockSpec((tm, LANES), lambda i: (i, 0)), small_spec],
        out_specs=(pl.BlockSpec((1, 1, 2 * tm), lambda i: (i, 0, 0)), small_spec, small_spec),
        compiler_params=_cparams(("arbitrary",)),
        name="dispatch_plan",
    )(route, counts)
    tile_expert = plan[:PLAN_NV_ROW].reshape(-1)[:n_tiles]
    tile_rows = rows.reshape(-1)[:n_tiles]
    n_valid = plan[PLAN_NV_ROW, 0:1]
    return dest, tile_expert, tile_rows, n_valid


def _row_copy(src_ref, src_row, dst_ref, dst_row, sem):
    src = src_ref.at[pl.ds(pl.multiple_of(src_row * ROW_TILE, ROW_TILE), ROW_TILE)]
    dst = dst_ref.at[pl.ds(pl.multiple_of(dst_row * ROW_TILE, ROW_TILE), ROW_TILE)]
    return pltpu.make_async_copy(src, dst, sem)


def _dispatch_kernel(dest_ref, h_ref, xs_in_ref, xs_ref, sem):
    del xs_in_ref
    tm = h_ref.shape[0] // ROW_TILE

    def issue(r, carry):
        _row_copy(h_ref, r, xs_ref, dest_ref[0, 0, r], sem).start()
        _row_copy(h_ref, r, xs_ref, dest_ref[0, 0, tm + r], sem).start()
        return carry

    lax.fori_loop(0, tm, issue, 0)

    def drain(r, carry):
        _row_copy(h_ref, 0, xs_ref, 0, sem).wait()
        _row_copy(h_ref, 0, xs_ref, 0, sem).wait()
        return carry

    lax.fori_loop(0, tm, drain, 0)


def _dispatch(hp, dest_tiles, n_rows):
    T = hp.shape[0] // ROW_TILE
    tm = TOK_TM
    xs0 = jnp.zeros((n_rows * ROW_TILE, LANES), hp.dtype)
    return pl.pallas_call(
        _dispatch_kernel,
        out_shape=jax.ShapeDtypeStruct(xs0.shape, hp.dtype),
        grid=(T // tm,),
        in_specs=[
            pl.BlockSpec((1, 1, 2 * tm), lambda i: (i, 0, 0), memory_space=pltpu.SMEM),
            pl.BlockSpec((tm * ROW_TILE, LANES), lambda i: (i, 0)),
            pl.BlockSpec(memory_space=pl.ANY),
        ],
        out_specs=pl.BlockSpec(memory_space=pl.ANY),
        scratch_shapes=[pltpu.SemaphoreType.DMA(())],
        input_output_aliases={2: 0},
        compiler_params=_cparams(("arbitrary",)),
        name="expert_dispatch",
    )(dest_tiles, hp, xs0)


def _gmm_kernel(te_ref, nv_ref, tr_ref, first_ref, par_ref, nxt_ref, *refs, layer, mode, tm, packed_in, packed_out):
    if mode == "swiglu":
        x_ref, w_hbm, a_ref, o_ref, wbuf_ref, wbf_ref, sems = refs
    else:
        x_ref, w_hbm, o_ref, wbuf_ref, wbf_ref, sems = refs
    j = pl.program_id(0)
    out_rows_per_row = ROW_TILE if packed_out else 1

    def weight_copy(e, s):
        return pltpu.make_async_copy(w_hbm.at[layer, e], wbuf_ref.at[s], sems.at[s])

    def compute(rows):
        x = _load_packed_rows(x_ref, rows).astype(BF16) if packed_in else x_ref[:rows, :]
        acc = jnp.dot(x, wbf_ref[...], preferred_element_type=F32)
        if mode == "swiglu":
            a = a_ref[:rows, :].astype(F32)
            acc = a * (1.0 / (1.0 + jnp.exp(-a))) * acc
        if packed_out:
            _store_packed_rows(o_ref, acc)
        else:
            o_ref[:rows, :] = acc.astype(o_ref.dtype)
        if rows < tm:
            rest = pl.ds(rows * out_rows_per_row, (tm - rows) * out_rows_per_row)
            o_ref[rest, :] = jnp.zeros(((tm - rows) * out_rows_per_row, o_ref.shape[1]), o_ref.dtype)

    @pl.when(j < nv_ref[0])
    def _():
        @pl.when(j == 0)
        def _():
            weight_copy(te_ref[0], 0).start()

        for s in range(2):
            @pl.when((first_ref[j] == 1) & (par_ref[j] == s))
            def _(s=s):
                weight_copy(te_ref[j], s).wait()

                @pl.when(nxt_ref[j] >= 0)
                def _():
                    weight_copy(nxt_ref[j], 1 - s).start()

                wbf_ref[...] = wbuf_ref[s].astype(BF16)

        @pl.when(tr_ref[j] > tm // 2)
        def _():
            compute(tm)

        @pl.when(tr_ref[j] <= tm // 2)
        def _():
            compute(tm // 2)

    @pl.when(j >= nv_ref[0])
    def _():
        o_ref[...] = jnp.zeros_like(o_ref)


def _gmm(x, w, layer, tile_plan, n_tiles, a=None, packed_out=False):
    K, N = w.shape[-2:]
    packed_in = x.dtype == U32
    tm = EXP_TM
    mode = "plain" if a is None else "swiglu"

    def row_map(j, te, nv, *_):
        return (jnp.minimum(j, nv[0] - 1), 0)

    x_block = (tm * ROW_TILE, LANES) if packed_in else (tm, K)
    out_block = (tm * ROW_TILE, LANES) if packed_out else (tm, N)
    in_specs = [pl.BlockSpec(x_block, row_map), pl.BlockSpec(memory_space=pl.ANY)]
    args = [x, w]
    if a is not None:
        in_specs.append(pl.BlockSpec((tm, N), row_map))
        args.append(a)
    grid_spec = pltpu.PrefetchScalarGridSpec(
        num_scalar_prefetch=len(tile_plan),
        grid=(n_tiles,),
        in_specs=in_specs,
        out_specs=pl.BlockSpec(out_block, lambda j, *_: (j, 0)),
        scratch_shapes=[pltpu.VMEM((2, K, N), F32), pltpu.VMEM((K, N), BF16), pltpu.SemaphoreType.DMA((2,))],
    )
    out_shape = (n_tiles * out_block[0], out_block[1])
    return pl.pallas_call(
        functools.partial(_gmm_kernel, layer=layer, mode=mode, tm=tm, packed_in=packed_in, packed_out=packed_out),
        out_shape=jax.ShapeDtypeStruct(out_shape, U32 if packed_out else BF16),
        grid_spec=grid_spec,
        compiler_params=_cparams(("arbitrary",)),
        name=f"expert_matmul_{mode}",
    )(*tile_plan, *args)


def _combine_kernel(dest_ref, next_ref, route_ref, hres_ref, y_ref, g_ref, b_ref, *refs, alpha, n_first):
    *out_refs, buf_ref, sems = refs
    o1_ref, o2_ref = out_refs[0], out_refs[-1]
    tm = hres_ref.shape[0]
    i = pl.program_id(0)
    n = pl.num_programs(0)

    def gather(d_ref, s):
        def issue(r, carry):
            _row_copy(y_ref, d_ref[0, 0, r], buf_ref.at[s, 0], r, sems.at[s]).start()
            _row_copy(y_ref, d_ref[0, 0, tm + r], buf_ref.at[s, 1], r, sems.at[s]).start()
            return carry
        lax.fori_loop(0, tm, issue, 0)

    def finish(s):
        def drain(r, carry):
            _row_copy(y_ref, 0, buf_ref.at[s, 0], 0, sems.at[s]).wait()
            _row_copy(y_ref, 0, buf_ref.at[s, 1], 0, sems.at[s]).wait()
            return carry

        lax.fori_loop(0, tm, drain, 0)
        rec = route_ref[...]
        f = (rec[:, R_G1:R_G1 + 1] * _load_packed_rows(buf_ref.at[s, 0], tm)
             + rec[:, R_G2:R_G2 + 1] * _load_packed_rows(buf_ref.at[s, 1], tm))
        h2 = _layer_norm_rows(alpha * hres_ref[...] + f, g_ref[...], b_ref[...])
        if n_first is None:
            o1_ref[...] = h2
        else:
            @pl.when(i < n_first)
            def _():
                o1_ref[...] = h2

            @pl.when(i >= n_first)
            def _():
                o2_ref[...] = h2

    @pl.when(i == 0)
    def _():
        gather(dest_ref, 0)

    for s in range(2):
        @pl.when(lax.rem(i, 2) == s)
        def _(s=s):
            @pl.when(i + 1 < n)
            def _():
                gather(next_ref, 1 - s)

            finish(s)


def _combine(dest_tiles, route, hres, y, g, b, alpha, split_rows=None):
    T, D = hres.shape
    tm = TOK_TM
    n_steps = T // tm
    if split_rows is None:
        n_first = None
        out_shape = (jax.ShapeDtypeStruct((T, D), F32),)
        out_specs = (pl.BlockSpec((tm, D), lambda i: (i, 0)),)
    else:
        assert split_rows % tm == 0
        n_first = split_rows // tm
        out_shape = (jax.ShapeDtypeStruct((split_rows, D), F32), jax.ShapeDtypeStruct((T - split_rows, D), F32))
        out_specs = (pl.BlockSpec((tm, D), lambda i: (jnp.minimum(i, n_first - 1), 0)),
                     pl.BlockSpec((tm, D), lambda i: (jnp.maximum(i - n_first, 0), 0)))
    return pl.pallas_call(
        functools.partial(_combine_kernel, alpha=alpha, n_first=n_first),
        out_shape=out_shape,
        grid=(n_steps,),
        in_specs=[
            pl.BlockSpec((1, 1, 2 * tm), lambda i: (i, 0, 0), memory_space=pltpu.SMEM),
            pl.BlockSpec((1, 1, 2 * tm), lambda i: (jnp.minimum(i + 1, n_steps - 1), 0, 0), memory_space=pltpu.SMEM),
            pl.BlockSpec((tm, LANES), lambda i: (i, 0)),
            pl.BlockSpec((tm, D), lambda i: (i, 0)),
            pl.BlockSpec(memory_space=pl.ANY),
            pl.BlockSpec((1, D), lambda i: (0, 0)),
            pl.BlockSpec((1, D), lambda i: (0, 0)),
        ],
        out_specs=out_specs,
        scratch_shapes=[pltpu.VMEM((2, 2, tm * ROW_TILE, LANES), U32), pltpu.SemaphoreType.DMA((2,))],
        compiler_params=_cparams(("arbitrary",)),
        name="expert_combine_ln",
    )(dest_tiles, dest_tiles, route, hres, y, g, b)


def _grouped_moe(h, hp, route, counts, w_gate, w_up, w_down, layer, g, b, alpha, split_rows=None):
    T, D = h.shape
    E = w_gate.shape[1]
    n_tiles = (T * 2) // EXP_TM + E
    dest_tiles, tile_expert, tile_rows, n_valid = _dispatch_plan(route, counts, E, n_tiles)
    jidx = jnp.arange(n_tiles, dtype=jnp.int32)
    first = ((tile_expert != jnp.roll(tile_expert, 1)) | (jidx == 0)) & (jidx < n_valid[0])
    parity = (jnp.cumsum(first.astype(jnp.int32)) - 1) % 2
    next_first = lax.cummin(jnp.where(first, jidx, n_tiles), reverse=True)
    next_first = jnp.concatenate([next_first[1:], jnp.full((1,), n_tiles, jnp.int32)])
    next_expert = jnp.where(next_first < n_tiles, tile_expert[jnp.minimum(next_first, n_tiles - 1)], -1)
    tile_plan = (tile_expert, n_valid, tile_rows, first.astype(jnp.int32), parity.astype(jnp.int32),
                 next_expert.astype(jnp.int32))
    xs = _dispatch(hp, dest_tiles, n_tiles * EXP_TM)
    a = _gmm(xs, w_gate, layer, tile_plan, n_tiles)
    hid = _gmm(xs, w_up, layer, tile_plan, n_tiles, a=a)
    y = _gmm(hid, w_down, layer, tile_plan, n_tiles, packed_out=True)
    return _combine(dest_tiles, route, h, y, g, b, alpha, split_rows)


def _fourier_a_kernel(x_ref, cs_ref, f_ref, y_ref, *, gd):
    n1 = x_ref.shape[0]
    for jj in range(SEQ_SUB):
        x = x_ref[:, jj, :].astype(BF16)
        zr, zi = [], []
        for gi in range(FOURIER_GROUPS):
            pq = jnp.dot(x[:, gi * gd:(gi + 1) * gd], cs_ref[...], preferred_element_type=F32)
            zr.append(pq[:, :gd])
            zi.append(pq[:, gd:])
        z = jnp.concatenate([jnp.concatenate(zr, axis=1), jnp.concatenate(zi, axis=1)], axis=0).astype(BF16)
        y = jnp.dot(f_ref[jj], z, preferred_element_type=F32)
        re = lax.bitcast_convert_type(y[:n1].astype(BF16).astype(F32), U32)
        im = lax.bitcast_convert_type(y[n1:].astype(BF16).astype(F32), U32)
        y_ref[:, jj, :] = lax.shift_right_logical(re, jnp.uint32(16)) | (im & jnp.uint32(0xFFFF0000))


def _fourier_b_kernel(y_ref, f_ref, o_ref, *, norm):
    n2 = o_ref.shape[0]
    for jj in range(SEQ_SUB):
        w = y_ref[jj * n2:(jj + 1) * n2, :]
        yr = lax.bitcast_convert_type(lax.shift_left(w, jnp.uint32(16)), F32)
        yi = lax.bitcast_convert_type(w & jnp.uint32(0xFFFF0000), F32)
        y = jnp.concatenate([yr, yi], axis=0).astype(BF16)
        o_ref[:, jj, :] = jnp.dot(f_ref[...], y, preferred_element_type=F32) * norm


def _dft_tables(S, gd):
    n1, n2 = S // SEQ_N2, SEQ_N2
    jc = (jnp.arange(gd)[:, None] * jnp.arange(gd)[None, :]) % gd
    ang = jc.astype(F32) * (2.0 * math.pi / gd)
    cs = jnp.concatenate([jnp.cos(ang), -jnp.sin(ang)], axis=1).astype(BF16)
    k1 = jnp.arange(n1)[None, :, None]
    s1 = jnp.arange(n1)[None, None, :]
    s2 = jnp.arange(n2)[:, None, None]
    th = ((k1 * (n2 * s1 + s2)) % S).astype(F32) * (2.0 * math.pi / S)
    fr, fi = jnp.cos(th), -jnp.sin(th)
    f1 = jnp.concatenate([jnp.concatenate([fr, -fi], axis=2), jnp.concatenate([fi, fr], axis=2)], axis=1).astype(BF16)
    kk = (jnp.arange(n2)[:, None] * jnp.arange(n2)[None, :]) % n2
    th2 = kk.astype(F32) * (2.0 * math.pi / n2)
    f2 = jnp.concatenate([jnp.cos(th2), jnp.sin(th2)], axis=1).astype(BF16)
    return cs, f1, f2


def _fourier_mixed(h, base, B, S):
    T, D = h.shape
    gd = D // FOURIER_GROUPS
    n1, n2 = S // SEQ_N2, SEQ_N2
    assert n1 % SEQ_SUB == 0 and n2 % SEQ_SUB == 0 and base % S == 0
    cs, f1, f2 = _dft_tables(S, gd)
    off = base // S
    y = pl.pallas_call(
        functools.partial(_fourier_a_kernel, gd=gd),
        out_shape=jax.ShapeDtypeStruct((B * n1, n2, D), U32),
        grid=(B, n2 // SEQ_SUB),
        in_specs=[
            pl.BlockSpec((n1, SEQ_SUB, D), lambda b, s: (off + b, s, 0)),
            pl.BlockSpec((gd, 2 * gd), lambda b, s: (0, 0)),
            pl.BlockSpec((SEQ_SUB, 2 * n1, 2 * n1), lambda b, s: (s, 0, 0)),
        ],
        out_specs=pl.BlockSpec((n1, SEQ_SUB, D), lambda b, s: (b, s, 0)),
        compiler_params=_cparams(("arbitrary", "arbitrary")),
        name="fourier_channel_step1",
    )(h.reshape(T // n2, n2, D), cs, f1)
    mixed = pl.pallas_call(
        functools.partial(_fourier_b_kernel, norm=1.0 / math.sqrt(S * gd)),
        out_shape=jax.ShapeDtypeStruct((B * n2, n1, D), F32),
        grid=(B, n1 // SEQ_SUB),
        in_specs=[
            pl.BlockSpec((SEQ_SUB * n2, D), lambda b, k: (b * (n1 // SEQ_SUB) + k, 0)),
            pl.BlockSpec((n2, 2 * n2), lambda b, k: (0, 0)),
        ],
        out_specs=pl.BlockSpec((n2, SEQ_SUB, D), lambda b, k: (b, k, 0)),
        compiler_params=_cparams(("arbitrary", "arbitrary")),
        name="fourier_step2",
    )(y.reshape(B * S, D), f2)
    return mixed.reshape(B * S, D)


def _reorder_qk_columns(w):
    D = w.shape[0]
    nblk = HEAD_DIM // ROT_HALF
    gap = ROT_PARTNER // ROT_HALF
    order = [0] + list(range(2, gap + 1)) + [1] + list(range(gap + 1, nblk))
    w6 = w.reshape(D, N_DIL_GROUPS, 3, HEADS_PER_GROUP, nblk, ROT_HALF)
    qk = w6[:, :, :2]
    qk = jnp.concatenate([qk[:, :, :, :, i:i + 1] for i in order], axis=4)
    return jnp.concatenate([qk, w6[:, :, 2:]], axis=2).reshape(w.shape)


def _rotary_tables(trunks, T):
    inv_freq = ROPE_THETA ** (-jnp.arange(0, ROT_DIM, 2, dtype=F32) / ROT_DIM)
    pos = jnp.concatenate([jnp.tile(jnp.arange(S, dtype=F32), B) for _, B, S in trunks])
    ang = pos[:, None] * inv_freq[None, :]
    cos, sin = jnp.cos(ang), jnp.sin(ang)
    gap = ROT_PARTNER - ROT_HALF
    cos_t = jnp.concatenate([cos, jnp.ones((T, gap), F32), cos, jnp.ones((T, gap), F32)], axis=1)
    sin_t = jnp.concatenate([-sin, jnp.zeros((T, gap), F32), sin, jnp.zeros((T, gap), F32)], axis=1)
    q_scale = HEAD_DIM ** -0.5
    cos3 = jnp.stack([cos_t * q_scale, cos_t, jnp.ones_like(cos_t)])
    sin3 = jnp.stack([sin_t * q_scale, sin_t, jnp.zeros_like(sin_t)])
    return cos3, sin3


def kernel(x_prompt, x_sample, ln_in_g, ln_in_b, attn_w_in, attn_w_out, fourier_w_out, router_w, router_b,
           expert_w_gate, expert_w_up, expert_w_down, ln_g, ln_b):
    Bp, Sp, D = x_prompt.shape
    Bs, Ss, _ = x_sample.shape
    depth = ln_g.shape[0]
    E = router_w.shape[1]
    alpha = (2 * depth) ** 0.25
    trunks = ((0, Bp, Sp), (Bp * Sp, Bs, Ss))
    T = Bp * Sp + Bs * Ss
    assert 2 * E == LANES and T % QKV_TM == 0
    assert depth == 2, "layer 0 (attention) is fused with the input LayerNorm, layer 1 is the Fourier layer"

    row = lambda v: v.reshape(1, -1).astype(F32)
    rw_hi = router_w.astype(BF16)
    rw_lo = (router_w - rw_hi.astype(F32)).astype(BF16)
    rw_cat = jnp.concatenate([rw_hi, rw_lo], axis=1)
    rb_pad = jnp.concatenate([router_b.astype(F32), jnp.zeros((LANES - E,), F32)]).reshape(1, LANES)
    cos3, sin3 = _rotary_tables(trunks, T)

    h, xn = _ln_in(x_prompt.reshape(Bp * Sp, D), x_sample.reshape(Bs * Ss, D), row(ln_in_g), row(ln_in_b))
    w_in = _reorder_qk_columns(attn_w_in[0]).astype(BF16)
    outs, lses = [], []
    for g, (_, dil) in enumerate(DILATION_PATTERNS):
        o, l = _attention_group(_qkv_group(xn, w_in, g, dil, cos3, sin3), dil, trunks)
        outs.append(o)
        lses.append(l)
    h, hp, route, counts = _mixer_tail(outs + lses, h, attn_w_out[0].astype(BF16),
                                       row(ln_g[0, 0]), row(ln_b[0, 0]), rw_cat, rb_pad, alpha, E)
    (h,) = _grouped_moe(h, hp, route, counts, expert_w_gate, expert_w_up, expert_w_down, 0,
                        row(ln_g[0, 1]), row(ln_b[0, 1]), alpha)

    mixed = [_fourier_mixed(h, base, B, S) for base, B, S in trunks]
    h, hp, route, counts = _mixer_tail(mixed, h, fourier_w_out[0].astype(BF16),
                                       row(ln_g[1, 0]), row(ln_b[1, 0]), rw_cat, rb_pad, alpha, E)
    y_p, y_s = _grouped_moe(h, hp, route, counts, expert_w_gate, expert_w_up, expert_w_down, 1,
                            row(ln_g[1, 1]), row(ln_b[1, 1]), alpha, split_rows=Bp * Sp)
    return (y_p.reshape(Bp, Sp, D), y_s.reshape(Bs, Ss, D))
```
